```python
import jax, jax.numpy as jnp
from jax import lax
import numpy as np

D_MODEL = 2048
BATCH = 2
SEQ = 4096
DEPTH = 1

CHUNK = 64
N_MEM = 256
MIX_WIDTH = D_MODEL
CONV_CH = MIX_WIDTH // 2
CONV_WIDTH = 31
SGU_CH = MIX_WIDTH - CONV_CH
SGU_GROUPS = 8
SGU_GROUP_DIM = SGU_CH // SGU_GROUPS
GMLP_CHUNK = 128
XA_HEADS = 4
XA_HEAD_DIM = D_MODEL // XA_HEADS
D_FF = 5632
MACARON_SCALE = 0.5
RMS_EPS = 1e-6
LN_EPS = 1e-5

kernel_name = "hybrid_conformer_conv_gmlp_macaron"


def rms_norm(x, g):
    xf = x.astype(jnp.float32)
    y = xf * lax.rsqrt(jnp.mean(xf * xf, axis=-1, keepdims=True) + RMS_EPS)
    return (y * g.astype(jnp.float32)).astype(x.dtype)


def layer_norm(x, g, b):
    xf = x.astype(jnp.float32)
    mu = jnp.mean(xf, axis=-1, keepdims=True)
    xc = xf - mu
    y = xc * lax.rsqrt(jnp.mean(xc * xc, axis=-1, keepdims=True) + LN_EPS)
    return (y * g.astype(jnp.float32) + b.astype(jnp.float32)).astype(x.dtype)


def swiglu_ffn(h, w_in, w_out):
    gate, up = jnp.split(h @ w_in, 2, axis=-1)
    return (jax.nn.silu(gate) * up) @ w_out


def conv_module_group(val, gate, conv_w, conv_b, ln_g, ln_b):
    a = val * jax.nn.sigmoid(gate)
    y = lax.conv_general_dilated(
        a, conv_w[:, None, :],
        window_strides=(1,),
        padding=[(CONV_WIDTH - 1, 0)],
        dimension_numbers=("NWC", "WIO", "NWC"),
        feature_group_count=CONV_CH)
    y = y + conv_b
    y = layer_norm(y, ln_g, ln_b)
    return jax.nn.silu(y)


def spatial_gating_group(u, v, ln_g, ln_b, w_s, b_s):
    B, S, _ = u.shape
    n_chunks = S // GMLP_CHUNK
    v = layer_norm(v, ln_g, ln_b)
    blk = jnp.arange(GMLP_CHUNK) // CHUNK
    mask = blk[None, :] <= blk[:, None]
    w = jnp.where(mask[None], w_s, jnp.zeros((), w_s.dtype))
    vh = v.reshape(B, n_chunks, GMLP_CHUNK, SGU_GROUPS, SGU_GROUP_DIM)
    mixed = jnp.einsum("hij,bnjhc->bnihc", w, vh)
    mixed = mixed + jnp.transpose(b_s)[None, None, :, :, None]
    return u * mixed.reshape(B, S, SGU_CH)


def memory_cross_attention(h, mem_n, w_q, w_kv, w_o):
    B, S, _ = h.shape
    M = mem_n.shape[1]
    q = (h @ w_q).reshape(B, S, XA_HEADS, XA_HEAD_DIM)
    k, v = jnp.split(mem_n @ w_kv, 2, axis=-1)
    k = k.reshape(B, M, XA_HEADS, XA_HEAD_DIM)
    v = v.reshape(B, M, XA_HEADS, XA_HEAD_DIM)
    s = jnp.einsum("bshd,bmhd->bhsm", q, k).astype(jnp.float32) * (XA_HEAD_DIM ** -0.5)
    p = jax.nn.softmax(s, axis=-1).astype(v.dtype)
    o = jnp.einsum("bhsm,bmhd->bshd", p, v).reshape(B, S, D_MODEL)
    return o @ w_o


def setup_inputs(seed: int = 0) -> dict:
    key = jax.random.key(seed)
    ks = jax.random.split(key, 32)
    L, D, F = DEPTH, D_MODEL, D_FF

    def nrm(k, shape, fan_in):
        return jax.random.normal(k, shape, jnp.float32) * (fan_in ** -0.5)

    def gain(k, shape):
        return 1.0 + 0.05 * jax.random.normal(k, shape, jnp.float32)

    def bias(k, shape, s=0.02):
        return s * jax.random.normal(k, shape, jnp.float32)

    return {
        "x": jax.random.normal(ks[0], (BATCH, SEQ, D), jnp.float32),
        "mem": jax.random.normal(ks[1], (BATCH, N_MEM, D), jnp.float32),
        "ffn1_norm": gain(ks[2], (L, D)),
        "ffn1_w_in": nrm(ks[3], (L, D, 2 * F), D),
        "ffn1_w_out": nrm(ks[4], (L, F, D), F),
        "mix_norm": gain(ks[5], (L, D)),
        "w_mix_in": nrm(ks[6], (L, D, 2 * CONV_CH + 2 * SGU_CH), D),
        "conv_w": nrm(ks[7], (L, CONV_WIDTH, CONV_CH), CONV_WIDTH),
        "conv_b": bias(ks[8], (L, CONV_CH)),
        "conv_ln_g": gain(ks[9], (L, CONV_CH)),
        "conv_ln_b": bias(ks[10], (L, CONV_CH)),
        "sgu_ln_g": gain(ks[11], (L, SGU_CH)),
        "sgu_ln_b": bias(ks[12], (L, SGU_CH)),
        "sgu_w": nrm(ks[13], (L, SGU_GROUPS, GMLP_CHUNK, GMLP_CHUNK), GMLP_CHUNK),
        "sgu_b": 1.0 + 0.1 * jax.random.normal(ks[14], (L, SGU_GROUPS, GMLP_CHUNK), jnp.float32),
        "out_norm_conv": gain(ks[15], (L, CONV_CH)),
        "out_norm_sgu": gain(ks[16], (L, SGU_CH)),
        "w_mix_out": nrm(ks[17], (L, MIX_WIDTH, D), MIX_WIDTH),
        "xattn_norm": gain(ks[18], (L, D)),
        "mem_norm": gain(ks[19], (L, D)),
        "w_q": nrm(ks[20], (L, D, D), D),
        "w_kv": nrm(ks[21], (L, D, 2 * D), D),
        "w_o": nrm(ks[22], (L, D, D), D),
        "ffn2_norm": gain(ks[23], (L, D)),
        "ffn2_w_in": nrm(ks[24], (L, D, 2 * F), D),
        "ffn2_w_out": nrm(ks[25], (L, F, D), F),
        "final_norm": gain(ks[26], (D,)),
    }


def reference(x, mem, ffn1_norm, ffn1_w_in, ffn1_w_out, mix_norm, w_mix_in,
              conv_w, conv_b, conv_ln_g, conv_ln_b, sgu_ln_g, sgu_ln_b, sgu_w, sgu_b,
              out_norm_conv, out_norm_sgu, w_mix_out, xattn_norm, mem_norm,
              w_q, w_kv, w_o, ffn2_norm, ffn2_w_in, ffn2_w_out, final_norm):
    split_at = [CONV_CH, 2 * CONV_CH, 2 * CONV_CH + SGU_CH]
    for l in range(DEPTH):
        h = rms_norm(x, ffn1_norm[l])
        x = x + MACARON_SCALE * swiglu_ffn(h, ffn1_w_in[l], ffn1_w_out[l])

        h = rms_norm(x, mix_norm[l])
        p = h @ w_mix_in[l]
        a_val, a_gate, g_u, g_v = jnp.split(p, split_at, axis=-1)
        ya = conv_module_group(a_val, a_gate, conv_w[l], conv_b[l], conv_ln_g[l], conv_ln_b[l])
        yb = spatial_gating_group(jax.nn.gelu(g_u, approximate=False),
                                  jax.nn.gelu(g_v, approximate=False),
                                  sgu_ln_g[l], sgu_ln_b[l], sgu_w[l], sgu_b[l])
        y = jnp.concatenate([rms_norm(ya, out_norm_conv[l]),
                             rms_norm(yb, out_norm_sgu[l])], axis=-1)
        x = x + y @ w_mix_out[l]

        h = rms_norm(x, xattn_norm[l])
        mem_n = rms_norm(mem, mem_norm[l])
        x = x + memory_cross_attention(h, mem_n, w_q[l], w_kv[l], w_o[l])

        h = rms_norm(x, ffn2_norm[l])
        x = x + MACARON_SCALE * swiglu_ffn(h, ffn2_w_in[l], ffn2_w_out[l])
    return rms_norm(x, final_norm)
```

```python
import functools

import jax
import jax.numpy as jnp
from jax import lax
from jax.experimental import pallas as pl
from jax.experimental.pallas import tpu as pltpu

D_MODEL = 2048
D_FF = 5632
CONV_CH = 1024
CONV_WIDTH = 31
SGU_CH = 1024
SGU_GROUPS = 8
SGU_GROUP_DIM = SGU_CH // SGU_GROUPS
GMLP_CHUNK = 128
CHUNK = 64
XA_HEADS = 4
XA_HEAD_DIM = D_MODEL // XA_HEADS
MACARON_SCALE = 0.5
RMS_EPS = 1e-6
LN_EPS = 1e-5

F32 = jnp.float32
BF16 = jnp.bfloat16

SUBLANES = 8
HALO = 32
VMEM_LIMIT = 60 * 1024 * 1024

FFN_TM, FFN_TF = 1024, 512
MIX_TM = 512
ATT_TM = 512
KV_TN = 512
CONV_ROWS = 32


def _rms(x, g):
    ms = jnp.mean(x * x, axis=-1, keepdims=True)
    return x * lax.rsqrt(ms + RMS_EPS) * g


def _layer_norm(x, g, b):
    mu = jnp.mean(x, axis=-1, keepdims=True)
    xc = x - mu
    var = jnp.mean(xc * xc, axis=-1, keepdims=True)
    return xc * lax.rsqrt(var + LN_EPS) * g + b


def _silu(x):
    return x * jax.nn.sigmoid(x)


def _gelu(x):
    return 0.5 * x * (1.0 + lax.erf(x * (0.5 ** 0.5)))


def _dot(a, b):
    return jnp.dot(a, b, preferred_element_type=F32)


def _for_row_chunks(n_rows, chunk, body):
    def step(r, carry):
        body(pl.ds(pl.multiple_of(r * chunk, chunk), chunk))
        return carry
    lax.fori_loop(0, n_rows // chunk, step, 0)


def _ffn_kernel(x_ref, g_ref, wg_ref, wu_ref, wo_ref, fin_ref, o_ref, h_ref, *, final_norm):
    j = pl.program_id(1)
    tm = x_ref.shape[0]

    @pl.when(j == 0)
    def _():
        def norm_rows(rows):
            x = x_ref[rows, :]
            h_ref[rows, :] = _rms(x, g_ref[...]).astype(BF16)
            o_ref[rows, :] = x
        _for_row_chunks(tm, 128, norm_rows)

    h = h_ref[...]
    gate = _dot(h, wg_ref[...])
    up = _dot(h, wu_ref[...])
    a = (MACARON_SCALE * _silu(gate) * up).astype(BF16)
    o_ref[...] += _dot(a, wo_ref[...])

    if final_norm:
        @pl.when(j == pl.num_programs(1) - 1)
        def _():
            def norm_rows(rows):
                o_ref[rows, :] = _rms(o_ref[rows, :], fin_ref[...])
            _for_row_chunks(tm, 128, norm_rows)


def _ffn(x, norm_g, w_in, w_out, fin_g, *, final_norm):
    t, d = x.shape
    f = w_out.shape[0]
    tm, tf = FFN_TM, FFN_TF
    nf = f // tf
    return pl.pallas_call(
        functools.partial(_ffn_kernel, final_norm=final_norm),
        out_shape=jax.ShapeDtypeStruct((t, d), F32),
        grid=(t // tm, nf),
        in_specs=[
            pl.BlockSpec((tm, d), lambda i, j: (i, 0)),
            pl.BlockSpec((1, d), lambda i, j: (0, 0)),
            pl.BlockSpec((d, tf), lambda i, j: (0, j)),
            pl.BlockSpec((d, tf), lambda i, j: (0, j + nf)),
            pl.BlockSpec((tf, d), lambda i, j: (j, 0)),
            pl.BlockSpec((1, d), lambda i, j: (0, 0)),
        ],
        out_specs=pl.BlockSpec((tm, d), lambda i, j: (i, 0)),
        scratch_shapes=[pltpu.VMEM((tm, d), BF16)],
        compiler_params=pltpu.CompilerParams(
            dimension_semantics=("arbitrary", "arbitrary"), vmem_limit_bytes=VMEM_LIMIT),
        name="ffn_final" if final_norm else "ffn",
    )(x, norm_g, w_in, w_in, w_out, fin_g)


def _conv_rows(ext_ref, w_ref, y_ref, rows):
    n = rows.size
    first = HALO - (CONV_WIDTH - 1)
    for c in range(CONV_CH // 128):
        lanes = pl.ds(c * 128, 128)
        acc = jnp.zeros((n, 128), F32)
        for r in range(SUBLANES):
            taps = [m for m in range(first, first + CONV_WIDTH) if m % SUBLANES == r]
            span = n if r == 0 else n + SUBLANES
            z = jnp.zeros((span, 128), F32)
            for m in taps:
                start = pl.multiple_of(rows.start + (m - r), SUBLANES)
                seg = ext_ref[pl.ds(start, span), lanes]
                z = z + w_ref[pl.ds(m - first, 1), lanes] * seg
            if r:
                z = pltpu.roll(z, span - r, 0)[:n]
            acc = acc + z
        y_ref[rows, lanes] = acc


def _mix_kernel(x_ref, g_ref, win_ref, wout_ref, convw_ref, convb_ref, clng_ref, clnb_ref,
                slng_ref, slnb_ref, sw_ref, sbias_ref, onc_ref, ons_ref,
                o_ref, h_ref, ext_ref, y_ref, vb_ref, *, blocks_per_seq):
    i = pl.program_id(0)
    j = pl.program_id(1)
    tm = x_ref.shape[0]

    @pl.when(j == 0)
    def _conv_branch():
        def norm_rows(rows):
            x = x_ref[rows, :]
            h_ref[rows, :] = _rms(x, g_ref[...]).astype(BF16)
            o_ref[rows, :] = x
        _for_row_chunks(tm, 128, norm_rows)

        @pl.when(i % blocks_per_seq == 0)
        def _():
            ext_ref[pl.ds(0, HALO), :] = jnp.zeros((HALO, CONV_CH), F32)

        p = _dot(h_ref[...], win_ref[...])
        ext_ref[pl.ds(HALO, tm), :] = p[:, :CONV_CH] * jax.nn.sigmoid(p[:, CONV_CH:])

        _for_row_chunks(tm, CONV_ROWS, functools.partial(_conv_rows, ext_ref, convw_ref, y_ref))
        ext_ref[pl.ds(0, HALO), :] = ext_ref[pl.ds(tm, HALO), :]

        def post_rows(rows):
            y = y_ref[rows, :] + convb_ref[...]
            y = _silu(_layer_norm(y, clng_ref[...], clnb_ref[...]))
            vb_ref[rows, :] = _rms(y, onc_ref[...]).astype(BF16)
        _for_row_chunks(tm, 128, post_rows)
        o_ref[...] += _dot(vb_ref[...], wout_ref[...])

    @pl.when(j == 1)
    def _sgu_branch():
        p = _dot(h_ref[...], win_ref[...])
        y_ref[...] = _gelu(p[:, :SGU_CH])

        v = _gelu(p[:, SGU_CH:])
        vb_ref[...] = _layer_norm(v, slng_ref[...], slnb_ref[...]).astype(BF16)

        blk = lax.broadcasted_iota(jnp.int32, (GMLP_CHUNK, GMLP_CHUNK), 0) // CHUNK
        blk_t = lax.broadcasted_iota(jnp.int32, (GMLP_CHUNK, GMLP_CHUNK), 1) // CHUNK
        mask = blk_t <= blk
        for hd in range(SGU_GROUPS):
            lanes = pl.ds(hd * SGU_GROUP_DIM, SGU_GROUP_DIM)
            w = jnp.where(mask, sw_ref[hd], 0.0).astype(BF16)
            for n in range(tm // GMLP_CHUNK):
                rows = pl.ds(n * GMLP_CHUNK, GMLP_CHUNK)
                mixed = _dot(w, vb_ref[rows, lanes]) + sbias_ref[:, lanes]
                y_ref[rows, lanes] = y_ref[rows, lanes] * mixed

        def post_rows(rows):
            vb_ref[rows, :] = _rms(y_ref[rows, :], ons_ref[...]).astype(BF16)
        _for_row_chunks(tm, 128, post_rows)
        o_ref[...] += _dot(vb_ref[...], wout_ref[...])


def _mix(x, norm_g, w_in, w_out, conv_w, conv_b, cln_g, cln_b, sln_g, sln_b, sgu_w, sgu_bias,
         on_conv, on_sgu, *, seq):
    t, d = x.shape
    tm = MIX_TM
    half = w_in.shape[1] // 2
    row = lambda n: pl.BlockSpec((1, n), lambda i, j: (0, 0))
    return pl.pallas_call(
        functools.partial(_mix_kernel, blocks_per_seq=seq // tm),
        out_shape=jax.ShapeDtypeStruct((t, d), F32),
        grid=(t // tm, 2),
        in_specs=[
            pl.BlockSpec((tm, d), lambda i, j: (i, 0)),
            row(d),
            pl.BlockSpec((d, half), lambda i, j: (0, j)),
            pl.BlockSpec((half // 2, d), lambda i, j: (j, 0)),
            pl.BlockSpec((CONV_WIDTH, CONV_CH), lambda i, j: (0, 0)),
            row(CONV_CH), row(CONV_CH), row(CONV_CH),
            row(SGU_CH), row(SGU_CH),
            pl.BlockSpec((SGU_GROUPS, GMLP_CHUNK, GMLP_CHUNK), lambda i, j: (0, 0, 0)),
            pl.BlockSpec((GMLP_CHUNK, SGU_CH), lambda i, j: (0, 0)),
            row(CONV_CH), row(SGU_CH),
        ],
        out_specs=pl.BlockSpec((tm, d), lambda i, j: (i, 0)),
        scratch_shapes=[
            pltpu.VMEM((tm, d), BF16),
            pltpu.VMEM((HALO + tm, CONV_CH), F32),
            pltpu.VMEM((tm, CONV_CH), F32),
            pltpu.VMEM((tm, CONV_CH), BF16),
        ],
        compiler_params=pltpu.CompilerParams(
            dimension_semantics=("arbitrary", "arbitrary"), vmem_limit_bytes=VMEM_LIMIT),
        name="mix",
    )(x, norm_g, w_in, w_out, conv_w, conv_b, cln_g, cln_b, sln_g, sln_b, sgu_w, sgu_bias,
      on_conv, on_sgu)


def _kv_kernel(m_ref, g_ref, w_ref, o_ref, h_ref):
    @pl.when(pl.program_id(0) == 0)
    def _():
        h_ref[...] = _rms(m_ref[...], g_ref[...]).astype(BF16)
    o_ref[...] = _dot(h_ref[...], w_ref[...]).astype(BF16)


def _kv(mem, norm_g, w_kv):
    m, d = mem.shape
    n = w_kv.shape[1]
    return pl.pallas_call(
        _kv_kernel,
        out_shape=jax.ShapeDtypeStruct((m, n), BF16),
        grid=(n // KV_TN,),
        in_specs=[
            pl.BlockSpec((m, d), lambda j: (0, 0)),
            pl.BlockSpec((1, d), lambda j: (0, 0)),
            pl.BlockSpec((d, KV_TN), lambda j: (0, j)),
        ],
        out_specs=pl.BlockSpec((m, KV_TN), lambda j: (0, j)),
        scratch_shapes=[pltpu.VMEM((m, d), BF16)],
        compiler_params=pltpu.CompilerParams(
            dimension_semantics=("arbitrary",), vmem_limit_bytes=VMEM_LIMIT),
        name="kv",
    )(mem, norm_g, w_kv)


def _attn_kernel(x_ref, g_ref, wq_ref, k_ref, v_ref, wo_ref, o_ref, h_ref):
    j = pl.program_id(1)
    tm = x_ref.shape[0]

    @pl.when(j == 0)
    def _():
        def norm_rows(rows):
            x = x_ref[rows, :]
            h_ref[rows, :] = _rms(x, g_ref[...]).astype(BF16)
            o_ref[rows, :] = x
        _for_row_chunks(tm, 128, norm_rows)

    q = _dot(h_ref[...], wq_ref[...]).astype(BF16)
    s = lax.dot_general(q, k_ref[0], (((1,), (1,)), ((), ())), preferred_element_type=F32)
    s = s * (XA_HEAD_DIM ** -0.5)
    e = jnp.exp(s - jnp.max(s, axis=-1, keepdims=True))
    p = (e / jnp.sum(e, axis=-1, keepdims=True)).astype(BF16)
    oh = _dot(p, v_ref[0]).astype(BF16)
    o_ref[...] += _dot(oh, wo_ref[...])


def _attn(x, norm_g, w_q, kv, w_o, *, seq):
    t, d = x.shape
    tm = ATT_TM
    bps = seq // tm
    n_mem = kv.shape[1]
    hd = XA_HEAD_DIM
    return pl.pallas_call(
        _attn_kernel,
        out_shape=jax.ShapeDtypeStruct((t, d), F32),
        grid=(t // tm, XA_HEADS),
        in_specs=[
            pl.BlockSpec((tm, d), lambda i, j: (i, 0)),
            pl.BlockSpec((1, d), lambda i, j: (0, 0)),
            pl.BlockSpec((d, hd), lambda i, j: (0, j)),
            pl.BlockSpec((1, n_mem, hd), lambda i, j: (i // bps, 0, j)),
            pl.BlockSpec((1, n_mem, hd), lambda i, j: (i // bps, 0, j + XA_HEADS)),
            pl.BlockSpec((hd, d), lambda i, j: (j, 0)),
        ],
        out_specs=pl.BlockSpec((tm, d), lambda i, j: (i, 0)),
        scratch_shapes=[pltpu.VMEM((tm, d), BF16)],
        compiler_params=pltpu.CompilerParams(
            dimension_semantics=("arbitrary", "arbitrary"), vmem_limit_bytes=VMEM_LIMIT),
        name="attn",
    )(x, norm_g, w_q, kv, kv, w_o)


def kernel(x, mem, ffn1_norm, ffn1_w_in, ffn1_w_out, mix_norm, w_mix_in, conv_w, conv_b, conv_ln_g, conv_ln_b, sgu_ln_g, sgu_ln_b, sgu_w, sgu_b, out_norm_conv, out_norm_sgu, w_mix_out, xattn_norm, mem_norm, w_q, w_kv, w_o, ffn2_norm, ffn2_w_in, ffn2_w_out, final_norm):
    b, s, d = x.shape
    n_mem = mem.shape[1]
    depth = ffn1_norm.shape[0]
    row = lambda v: v.reshape(1, -1)
    ones = jnp.ones((1, d), F32)

    xs = x.reshape(b * s, d)
    mem2 = mem.reshape(b * n_mem, d)
    for l in range(depth):
        last = l == depth - 1
        xs = _ffn(xs, row(ffn1_norm[l]), ffn1_w_in[l].astype(BF16), ffn1_w_out[l].astype(BF16),
                  ones, final_norm=False)
        sgu_bias = jnp.repeat(jnp.transpose(sgu_b[l]), SGU_GROUP_DIM, axis=1)
        xs = _mix(xs, row(mix_norm[l]), w_mix_in[l].astype(BF16), w_mix_out[l].astype(BF16),
                  conv_w[l], row(conv_b[l]), row(conv_ln_g[l]), row(conv_ln_b[l]),
                  row(sgu_ln_g[l]), row(sgu_ln_b[l]), sgu_w[l], sgu_bias,
                  row(out_norm_conv[l]), row(out_norm_sgu[l]), seq=s)
        kv = _kv(mem2, row(mem_norm[l]), w_kv[l].astype(BF16)).reshape(b, n_mem, 2 * d)
        xs = _attn(xs, row(xattn_norm[l]), w_q[l].astype(BF16), kv, w_o[l].astype(BF16), seq=s)
        xs = _ffn(xs, row(ffn2_norm[l]), ffn2_w_in[l].astype(BF16), ffn2_w_out[l].astype(BF16),
                  row(final_norm) if last else ones, final_norm=last)
    return xs.reshape(b, s, d)
```

```python
import functools

import jax
import jax.numpy as jnp
from jax import lax
from jax.experimental import pallas as pl
from jax.experimental.pallas import tpu as pltpu

D_MODEL = 2048
D_FF = 5632
CONV_CH = 1024
CONV_WIDTH = 31
SGU_CH = 1024
SGU_GROUPS = 8
SGU_GROUP_DIM = SGU_CH // SGU_GROUPS
GMLP_CHUNK = 128
CHUNK = 64
XA_HEADS = 4
XA_HEAD_DIM = D_MODEL // XA_HEADS
MACARON_SCALE = 0.5
RMS_EPS = 1e-6
LN_EPS = 1e-5

F32 = jnp.float32
BF16 = jnp.bfloat16

SUBLANES = 8
HALO = 32
VMEM_LIMIT = 60 * 1024 * 1024

FFN_TM, FFN_TF = 1024, 512
MIX_TM = 512
ATT_TM = 512
KV_TN = 512
CONV_ROWS = 32
ROW_CHUNK = 128


def _rms(x, g):
    ms = jnp.mean(x * x, axis=-1, keepdims=True)
    return x * lax.rsqrt(ms + RMS_EPS) * g


def _layer_norm(x, g, b):
    mu = jnp.mean(x, axis=-1, keepdims=True)
    xc = x - mu
    var = jnp.mean(xc * xc, axis=-1, keepdims=True)
    return xc * lax.rsqrt(var + LN_EPS) * g + b


def _silu(x):
    return x * jax.nn.sigmoid(x)


def _gelu(x):
    return 0.5 * x * (1.0 + lax.erf(x * (0.5 ** 0.5)))


def _dot(a, b):
    return jnp.dot(a, b, preferred_element_type=F32)


def _for_row_chunks(n_rows, chunk, body):
    def step(r, carry):
        body(pl.ds(pl.multiple_of(r * chunk, chunk), chunk))
        return carry
    lax.fori_loop(0, n_rows // chunk, step, 0)


def _ffn_kernel(x_hbm, g_ref, wg_ref, wu_ref, wo_ref, fin_ref, o_ref, h_ref, sem, *, final_norm):
    i = pl.program_id(0)
    j = pl.program_id(1)
    tm = o_ref.shape[0]

    @pl.when(j == 0)
    def _():
        def chunk_copy(r):
            src = x_hbm.at[pl.ds(pl.multiple_of(i * tm + r * ROW_CHUNK, ROW_CHUNK), ROW_CHUNK)]
            dst = o_ref.at[pl.ds(pl.multiple_of(r * ROW_CHUNK, ROW_CHUNK), ROW_CHUNK)]
            return pltpu.make_async_copy(src, dst, sem.at[r])

        def start(r, carry):
            chunk_copy(r).start()
            return carry
        lax.fori_loop(0, tm // ROW_CHUNK, start, 0)

        def norm_rows(r, carry):
            chunk_copy(r).wait()
            rows = pl.ds(pl.multiple_of(r * ROW_CHUNK, ROW_CHUNK), ROW_CHUNK)
            h_ref[rows, :] = _rms(o_ref[rows, :], g_ref[...]).astype(BF16)
            return carry
        lax.fori_loop(0, tm // ROW_CHUNK, norm_rows, 0)

    h = h_ref[...]
    gate = _dot(h, wg_ref[...].astype(BF16))
    up = _dot(h, wu_ref[...].astype(BF16))
    a = (MACARON_SCALE * _silu(gate) * up).astype(BF16)
    o_ref[...] += _dot(a, wo_ref[...].astype(BF16))

    if final_norm:
        @pl.when(j == pl.num_programs(1) - 1)
        def _():
            def norm_rows(rows):
                o_ref[rows, :] = _rms(o_ref[rows, :], fin_ref[...])
            _for_row_chunks(tm, ROW_CHUNK, norm_rows)


def _ffn(x, norm_g, w_in, w_out, fin_g, *, final_norm):
    t, d = x.shape
    f = w_out.shape[0]
    tm, tf = FFN_TM, FFN_TF
    nf = f // tf
    return pl.pallas_call(
        functools.partial(_ffn_kernel, final_norm=final_norm),
        out_shape=jax.ShapeDtypeStruct((t, d), F32),
        grid=(t // tm, nf),
        in_specs=[
            pl.BlockSpec(memory_space=pl.ANY),
            pl.BlockSpec((1, d), lambda i, j: (0, 0)),
            pl.BlockSpec((d, tf), lambda i, j: (0, j)),
            pl.BlockSpec((d, tf), lambda i, j: (0, j + nf)),
            pl.BlockSpec((tf, d), lambda i, j: (j, 0)),
            pl.BlockSpec((1, d), lambda i, j: (0, 0)),
        ],
        out_specs=pl.BlockSpec((tm, d), lambda i, j: (i, 0)),
        scratch_shapes=[pltpu.VMEM((tm, d), BF16), pltpu.SemaphoreType.DMA((tm // ROW_CHUNK,))],
        compiler_params=pltpu.CompilerParams(
            dimension_semantics=("arbitrary", "arbitrary"), vmem_limit_bytes=VMEM_LIMIT),
        name="ffn_final" if final_norm else "ffn",
    )(x, norm_g, w_in, w_in, w_out, fin_g)


def _conv_rows(ext_ref, w_ref, y_ref, rows):
    n = rows.size
    first = HALO - (CONV_WIDTH - 1)
    for c in range(CONV_CH // 128):
        lanes = pl.ds(c * 128, 128)
        acc = jnp.zeros((n, 128), F32)
        for r in range(SUBLANES):
            taps = [m for m in range(first, first + CONV_WIDTH) if m % SUBLANES == r]
            span = n if r == 0 else n + SUBLANES
            z = jnp.zeros((span, 128), F32)
            for m in taps:
                start = pl.multiple_of(rows.start + (m - r), SUBLANES)
                seg = ext_ref[pl.ds(start, span), lanes]
                z = z + w_ref[pl.ds(m - first, 1), lanes] * seg
            if r:
                z = pltpu.roll(z, span - r, 0)[:n]
            acc = acc + z
        y_ref[rows, lanes] = acc


def _mix_kernel(x_ref, g_ref, win_ref, wout_ref, convw_ref, convb_ref, clng_ref, clnb_ref,
                slng_ref, slnb_ref, sw_ref, sbias_ref, onc_ref, ons_ref,
                o_ref, h_ref, ext_ref, y_ref, vb_ref, *, blocks_per_seq):
    i = pl.program_id(0)
    j = pl.program_id(1)
    tm = x_ref.shape[0]

    @pl.when(j == 0)
    def _conv_branch():
        def norm_rows(rows):
            x = x_ref[rows, :]
            h_ref[rows, :] = _rms(x, g_ref[...]).astype(BF16)
            o_ref[rows, :] = x
        _for_row_chunks(tm, 128, norm_rows)

        @pl.when(i % blocks_per_seq == 0)
        def _():
            ext_ref[pl.ds(0, HALO), :] = jnp.zeros((HALO, CONV_CH), F32)

        p = _dot(h_ref[...], win_ref[...])
        ext_ref[pl.ds(HALO, tm), :] = p[:, :CONV_CH] * jax.nn.sigmoid(p[:, CONV_CH:])

        _for_row_chunks(tm, CONV_ROWS, functools.partial(_conv_rows, ext_ref, convw_ref, y_ref))
        ext_ref[pl.ds(0, HALO), :] = ext_ref[pl.ds(tm, HALO), :]

        def post_rows(rows):
            y = y_ref[rows, :] + convb_ref[...]
            y = _silu(_layer_norm(y, clng_ref[...], clnb_ref[...]))
            vb_ref[rows, :] = _rms(y, onc_ref[...]).astype(BF16)
        _for_row_chunks(tm, 128, post_rows)
        o_ref[...] += _dot(vb_ref[...], wout_ref[...])

    @pl.when(j == 1)
    def _sgu_branch():
        p = _dot(h_ref[...], win_ref[...])
        y_ref[...] = _gelu(p[:, :SGU_CH])

        v = _gelu(p[:, SGU_CH:])
        vb_ref[...] = _layer_norm(v, slng_ref[...], slnb_ref[...]).astype(BF16)

        blk = lax.broadcasted_iota(jnp.int32, (GMLP_CHUNK, GMLP_CHUNK), 0) // CHUNK
        blk_t = lax.broadcasted_iota(jnp.int32, (GMLP_CHUNK, GMLP_CHUNK), 1) // CHUNK
        mask = blk_t <= blk
        for hd in range(SGU_GROUPS):
            lanes = pl.ds(hd * SGU_GROUP_DIM, SGU_GROUP_DIM)
            w = jnp.where(mask, sw_ref[hd], 0.0).astype(BF16)
            for n in range(tm // GMLP_CHUNK):
                rows = pl.ds(n * GMLP_CHUNK, GMLP_CHUNK)
                mixed = _dot(w, vb_ref[rows, lanes]) + sbias_ref[:, lanes]
                y_ref[rows, lanes] = y_ref[rows, lanes] * mixed

        def post_rows(rows):
            vb_ref[rows, :] = _rms(y_ref[rows, :], ons_ref[...]).astype(BF16)
        _for_row_chunks(tm, 128, post_rows)
        o_ref[...] += _dot(vb_ref[...], wout_ref[...])


def _mix(x, norm_g, w_in, w_out, conv_w, conv_b, cln_g, cln_b, sln_g, sln_b, sgu_w, sgu_bias,
         on_conv, on_sgu, *, seq):
    t, d = x.shape
    tm = MIX_TM
    half = w_in.shape[1] // 2
    row = lambda n: pl.BlockSpec((1, n), lambda i, j: (0, 0))
    return pl.pallas_call(
        functools.partial(_mix_kernel, blocks_per_seq=seq // tm),
        out_shape=jax.ShapeDtypeStruct((t, d), F32),
        grid=(t // tm, 2),
        in_specs=[
            pl.BlockSpec((tm, d), lambda i, j: (i, 0)),
            row(d),
            pl.BlockSpec((d, half), lambda i, j: (0, j)),
            pl.BlockSpec((half // 2, d), lambda i, j: (j, 0)),
            pl.BlockSpec((CONV_WIDTH, CONV_CH), lambda i, j: (0, 0)),
            row(CONV_CH), row(CONV_CH), row(CONV_CH),
            row(SGU_CH), row(SGU_CH),
            pl.BlockSpec((SGU_GROUPS, GMLP_CHUNK, GMLP_CHUNK), lambda i, j: (0, 0, 0)),
            pl.BlockSpec((GMLP_CHUNK, SGU_CH), lambda i, j: (0, 0)),
            row(CONV_CH), row(SGU_CH),
        ],
        out_specs=pl.BlockSpec((tm, d), lambda i, j: (i, 0)),
        scratch_shapes=[
            pltpu.VMEM((tm, d), BF16),
            pltpu.VMEM((HALO + tm, CONV_CH), F32),
            pltpu.VMEM((tm, CONV_CH), F32),
            pltpu.VMEM((tm, CONV_CH), BF16),
        ],
        compiler_params=pltpu.CompilerParams(
            dimension_semantics=("arbitrary", "arbitrary"), vmem_limit_bytes=VMEM_LIMIT),
        name="mix",
    )(x, norm_g, w_in, w_out, conv_w, conv_b, cln_g, cln_b, sln_g, sln_b, sgu_w, sgu_bias,
      on_conv, on_sgu)


def _kv_kernel(m_ref, g_ref, w_ref, o_ref, h_ref):
    @pl.when(pl.program_id(0) == 0)
    def _():
        h_ref[...] = _rms(m_ref[...], g_ref[...]).astype(BF16)
    o_ref[...] = _dot(h_ref[...], w_ref[...].astype(BF16)).astype(BF16)


def _kv(mem, norm_g, w_kv):
    m, d = mem.shape
    n = w_kv.shape[1]
    return pl.pallas_call(
        _kv_kernel,
        out_shape=jax.ShapeDtypeStruct((m, n), BF16),
        grid=(n // KV_TN,),
        in_specs=[
            pl.BlockSpec((m, d), lambda j: (0, 0)),
            pl.BlockSpec((1, d), lambda j: (0, 0)),
            pl.BlockSpec((d, KV_TN), lambda j: (0, j)),
        ],
        out_specs=pl.BlockSpec((m, KV_TN), lambda j: (0, j)),
        scratch_shapes=[pltpu.VMEM((m, d), BF16)],
        compiler_params=pltpu.CompilerParams(
            dimension_semantics=("arbitrary",), vmem_limit_bytes=VMEM_LIMIT),
        name="kv",
    )(mem, norm_g, w_kv)


def _attn_kernel(x_ref, g_ref, wq_ref, k_ref, v_ref, wo_ref, o_ref, h_ref):
    j = pl.program_id(1)
    tm = x_ref.shape[0]

    @pl.when(j == 0)
    def _():
        def norm_rows(rows):
            x = x_ref[rows, :]
            h_ref[rows, :] = _rms(x, g_ref[...]).astype(BF16)
            o_ref[rows, :] = x
        _for_row_chunks(tm, 128, norm_rows)

    q = _dot(h_ref[...], wq_ref[...]).astype(BF16)
    s = lax.dot_general(q, k_ref[0], (((1,), (1,)), ((), ())), preferred_element_type=F32)
    s = s * (XA_HEAD_DIM ** -0.5)
    e = jnp.exp(s - jnp.max(s, axis=-1, keepdims=True))
    p = (e / jnp.sum(e, axis=-1, keepdims=True)).astype(BF16)
    oh = _dot(p, v_ref[0]).astype(BF16)
    o_ref[...] += _dot(oh, wo_ref[...])


def _attn(x, norm_g, w_q, kv, w_o, *, seq):
    t, d = x.shape
    tm = ATT_TM
    bps = seq // tm
    n_mem = kv.shape[1]
    hd = XA_HEAD_DIM
    return pl.pallas_call(
        _attn_kernel,
        out_shape=jax.ShapeDtypeStruct((t, d), F32),
        grid=(t // tm, XA_HEADS),
        in_specs=[
            pl.BlockSpec((tm, d), lambda i, j: (i, 0)),
            pl.BlockSpec((1, d), lambda i, j: (0, 0)),
            pl.BlockSpec((d, hd), lambda i, j: (0, j)),
            pl.BlockSpec((1, n_mem, hd), lambda i, j: (i // bps, 0, j)),
            pl.BlockSpec((1, n_mem, hd), lambda i, j: (i // bps, 0, j + XA_HEADS)),
            pl.BlockSpec((hd, d), lambda i, j: (j, 0)),
        ],
        out_specs=pl.BlockSpec((tm, d), lambda i, j: (i, 0)),
        scratch_shapes=[pltpu.VMEM((tm, d), BF16)],
        compiler_params=pltpu.CompilerParams(
            dimension_semantics=("arbitrary", "arbitrary"), vmem_limit_bytes=VMEM_LIMIT),
        name="attn",
    )(x, norm_g, w_q, kv, kv, w_o)


def kernel(x, mem, ffn1_norm, ffn1_w_in, ffn1_w_out, mix_norm, w_mix_in, conv_w, conv_b, conv_ln_g, conv_ln_b, sgu_ln_g, sgu_ln_b, sgu_w, sgu_b, out_norm_conv, out_norm_sgu, w_mix_out, xattn_norm, mem_norm, w_q, w_kv, w_o, ffn2_norm, ffn2_w_in, ffn2_w_out, final_norm):
    b, s, d = x.shape
    n_mem = mem.shape[1]
    depth = ffn1_norm.shape[0]
    row = lambda v: v.reshape(1, -1)
    ones = jnp.ones((1, d), F32)

    xs = x.reshape(b * s, d)
    mem2 = mem.reshape(b * n_mem, d)
    for l in range(depth):
        last = l == depth - 1
        xs = _ffn(xs, row(ffn1_norm[l]), ffn1_w_in[l], ffn1_w_out[l], ones, final_norm=False)
        sgu_bias = jnp.repeat(jnp.transpose(sgu_b[l]), SGU_GROUP_DIM, axis=1)
        xs = _mix(xs, row(mix_norm[l]), w_mix_in[l].astype(BF16), w_mix_out[l].astype(BF16),
                  conv_w[l], row(conv_b[l]), row(conv_ln_g[l]), row(conv_ln_b[l]),
                  row(sgu_ln_g[l]), row(sgu_ln_b[l]), sgu_w[l], sgu_bias,
                  row(out_norm_conv[l]), row(out_norm_sgu[l]), seq=s)
        kv = _kv(mem2, row(mem_norm[l]), w_kv[l]).reshape(b, n_mem, 2 * d)
        xs = _attn(xs, row(xattn_norm[l]), w_q[l].astype(BF16), kv, w_o[l].astype(BF16), seq=s)
        xs = _ffn(xs, row(ffn2_norm[l]), ffn2_w_in[l], ffn2_w_out[l],
                  row(final_norm) if last else ones, final_norm=last)
    return xs.reshape(b, s, d)
```

```python
import functools

import jax
import jax.numpy as jnp
from jax import lax
from jax.experimental import pallas as pl
from jax.experimental.pallas import tpu as pltpu

D_MODEL = 2048
D_FF = 5632
CONV_CH = 1024
CONV_WIDTH = 31
SGU_CH = 1024
SGU_GROUPS = 8
SGU_GROUP_DIM = SGU_CH // SGU_GROUPS
GMLP_CHUNK = 128
CHUNK = 64
XA_HEADS = 4
XA_HEAD_DIM = D_MODEL // XA_HEADS
MACARON_SCALE = 0.5
RMS_EPS = 1e-6
LN_EPS = 1e-5

F32 = jnp.float32
BF16 = jnp.bfloat16

SUBLANES = 8
HALO = 32
VMEM_LIMIT = 62 * 1024 * 1024

FFN_TM, FFN_TF = 1024, 512
MIX_TM = 512
ATT_TM = 512
KV_TN = 512
CONV_ROWS = 64
ROW_CHUNK = 128
WEIGHT_SLOTS = 3
WEIGHT_AHEAD = WEIGHT_SLOTS - 1


def _rms(x, g):
    ms = jnp.mean(x * x, axis=-1, keepdims=True)
    return x * lax.rsqrt(ms + RMS_EPS) * g


def _layer_norm(x, g, b):
    mu = jnp.mean(x, axis=-1, keepdims=True)
    xc = x - mu
    var = jnp.mean(xc * xc, axis=-1, keepdims=True)
    return xc * lax.rsqrt(var + LN_EPS) * g + b


def _silu(x):
    return x * jax.nn.sigmoid(x)


def _gelu(x):
    return 0.5 * x * (1.0 + lax.erf(x * (0.5 ** 0.5)))


def _dot(a, b):
    return jnp.dot(a, b, preferred_element_type=F32)


def _for_row_chunks(n_rows, chunk, body):
    def step(r, carry):
        body(pl.ds(pl.multiple_of(r * chunk, chunk), chunk))
        return carry
    lax.fori_loop(0, n_rows // chunk, step, 0)


def _load_rows_and_normalise(x_hbm, row0, o_ref, h_ref, g_ref, sem):
    n_chunks = o_ref.shape[0] // ROW_CHUNK

    def chunk_copy(r):
        src = x_hbm.at[pl.ds(pl.multiple_of(row0 + r * ROW_CHUNK, ROW_CHUNK), ROW_CHUNK)]
        dst = o_ref.at[pl.ds(pl.multiple_of(r * ROW_CHUNK, ROW_CHUNK), ROW_CHUNK)]
        return pltpu.make_async_copy(src, dst, sem.at[r])

    def start(r, carry):
        chunk_copy(r).start()
        return carry
    lax.fori_loop(0, n_chunks, start, 0)

    def norm_rows(r, carry):
        chunk_copy(r).wait()
        rows = pl.ds(pl.multiple_of(r * ROW_CHUNK, ROW_CHUNK), ROW_CHUNK)
        h_ref[rows, :] = _rms(o_ref[rows, :], g_ref[...]).astype(BF16)
        return carry
    lax.fori_loop(0, n_chunks, norm_rows, 0)


def _ffn_kernel(x_hbm, g_ref, win_hbm, wout_hbm, fin_ref, o_ref,
                h_ref, gbuf, ubuf, obuf, xsem, wsem, *, final_norm):
    i = pl.program_id(0)
    j = pl.program_id(1)
    n_j = pl.num_programs(1)
    tm = o_ref.shape[0]
    d_ff, tf = wout_hbm.shape[0], gbuf.shape[2]
    step = i * n_j + j
    slot = step % WEIGHT_SLOTS
    last_step = pl.num_programs(0) * n_j - 1

    def weight_copies(t, slot):
        col = pl.multiple_of((t % n_j) * tf, tf)
        up_col = pl.multiple_of(d_ff + col, tf)
        return (
            pltpu.make_async_copy(win_hbm.at[:, pl.ds(col, tf)], gbuf.at[slot], wsem.at[0, slot]),
            pltpu.make_async_copy(win_hbm.at[:, pl.ds(up_col, tf)], ubuf.at[slot], wsem.at[1, slot]),
            pltpu.make_async_copy(wout_hbm.at[pl.ds(col, tf), :], obuf.at[slot], wsem.at[2, slot]),
        )

    @pl.when(step == 0)
    def _():
        for t in range(WEIGHT_AHEAD):
            for c in weight_copies(t, t):
                c.start()

    for c in weight_copies(step + WEIGHT_AHEAD, (step + WEIGHT_AHEAD) % WEIGHT_SLOTS):
        c.start()

    @pl.when(j == 0)
    def _():
        _load_rows_and_normalise(x_hbm, i * tm, o_ref, h_ref, g_ref, xsem)

    for c in weight_copies(step, slot):
        c.wait()

    h = h_ref[...]
    gate = _dot(h, gbuf[slot].astype(BF16))
    up = _dot(h, ubuf[slot].astype(BF16))
    a = (MACARON_SCALE * _silu(gate) * up).astype(BF16)
    o_ref[...] += _dot(a, obuf[slot].astype(BF16))

    if final_norm:
        @pl.when(j == n_j - 1)
        def _():
            def norm_rows(rows):
                o_ref[rows, :] = _rms(o_ref[rows, :], fin_ref[...])
            _for_row_chunks(tm, ROW_CHUNK, norm_rows)

    @pl.when(step == last_step)
    def _():
        for t in range(WEIGHT_AHEAD):
            for c in weight_copies(step + 1 + t, (step + 1 + t) % WEIGHT_SLOTS):
                c.wait()


def _ffn(x, norm_g, w_in, w_out, fin_g, *, final_norm):
    t, d = x.shape
    f = w_out.shape[0]
    tm, tf = FFN_TM, FFN_TF
    nf = f // tf
    return pl.pallas_call(
        functools.partial(_ffn_kernel, final_norm=final_norm),
        out_shape=jax.ShapeDtypeStruct((t, d), F32),
        grid=(t // tm, nf),
        in_specs=[
            pl.BlockSpec(memory_space=pl.ANY),
            pl.BlockSpec((1, d), lambda i, j: (0, 0)),
            pl.BlockSpec(memory_space=pl.ANY),
            pl.BlockSpec(memory_space=pl.ANY),
            pl.BlockSpec((1, d), lambda i, j: (0, 0)),
        ],
        out_specs=pl.BlockSpec((tm, d), lambda i, j: (i, 0)),
        scratch_shapes=[
            pltpu.VMEM((tm, d), BF16),
            pltpu.VMEM((WEIGHT_SLOTS, d, tf), F32),
            pltpu.VMEM((WEIGHT_SLOTS, d, tf), F32),
            pltpu.VMEM((WEIGHT_SLOTS, tf, d), F32),
            pltpu.SemaphoreType.DMA((tm // ROW_CHUNK,)),
            pltpu.SemaphoreType.DMA((3, WEIGHT_SLOTS)),
        ],
        compiler_params=pltpu.CompilerParams(
            dimension_semantics=("arbitrary", "arbitrary"), vmem_limit_bytes=VMEM_LIMIT),
        name="ffn_final" if final_norm else "ffn",
    )(x, norm_g, w_in, w_out, fin_g)


def _conv_rows(ext_ref, w_ref, y_ref, rows):
    n = rows.size
    first = HALO - (CONV_WIDTH - 1)
    for c in range(CONV_CH // 128):
        lanes = pl.ds(c * 128, 128)
        acc = jnp.zeros((n, 128), F32)
        for r in range(SUBLANES):
            taps = [m for m in range(first, first + CONV_WIDTH) if m % SUBLANES == r]
            span = n if r == 0 else n + SUBLANES
            z = jnp.zeros((span, 128), F32)
            for m in taps:
                seg = ext_ref[pl.ds(rows.start + (m - r), span), lanes]
                z = z + w_ref[pl.ds(m - first, 1), lanes] * seg
            if r:
                z = pltpu.roll(z, span - r, 0)[:n]
            acc = acc + z
        y_ref[rows, lanes] = acc


def _mix_kernel(x_hbm, g_ref, win_ref, wout_ref, convw_ref, convb_ref, clng_ref, clnb_ref,
                slng_ref, slnb_ref, sw_ref, sbias_ref, onc_ref, ons_ref,
                o_ref, h_ref, ext_ref, y_ref, vb_ref, xsem, *, blocks_per_seq):
    i = pl.program_id(0)
    tm = o_ref.shape[0]

    _load_rows_and_normalise(x_hbm, i * tm, o_ref, h_ref, g_ref, xsem)

    @pl.when(i % blocks_per_seq == 0)
    def _():
        ext_ref[pl.ds(0, HALO), :] = jnp.zeros((HALO, CONV_CH), F32)

    h = h_ref[...]
    p = _dot(h, win_ref[:, pl.ds(0, 2 * CONV_CH)])
    ext_ref[pl.ds(HALO, tm), :] = p[:, :CONV_CH] * jax.nn.sigmoid(p[:, CONV_CH:])
    q = _dot(h, win_ref[:, pl.ds(2 * CONV_CH, 2 * SGU_CH)])

    for r in range(tm // CONV_ROWS):
        _conv_rows(ext_ref, convw_ref, y_ref, pl.ds(r * CONV_ROWS, CONV_ROWS))
    ext_ref[pl.ds(0, HALO), :] = ext_ref[pl.ds(tm, HALO), :]

    y = y_ref[...] + convb_ref[...]
    y = _silu(_layer_norm(y, clng_ref[...], clnb_ref[...]))
    vb_ref[:, pl.ds(0, CONV_CH)] = _rms(y, onc_ref[...]).astype(BF16)

    u = _gelu(q[:, :SGU_CH])
    v = _layer_norm(_gelu(q[:, SGU_CH:]), slng_ref[...], slnb_ref[...]).astype(BF16)
    blk = lax.broadcasted_iota(jnp.int32, (GMLP_CHUNK, GMLP_CHUNK), 0) // CHUNK
    blk_t = lax.broadcasted_iota(jnp.int32, (GMLP_CHUNK, GMLP_CHUNK), 1) // CHUNK
    mask = blk_t <= blk
    ws = [jnp.where(mask, sw_ref[hd], 0.0).astype(BF16) for hd in range(SGU_GROUPS)]
    for n in range(tm // GMLP_CHUNK):
        r0 = n * GMLP_CHUNK
        tiles = []
        for hd in range(SGU_GROUPS):
            c0 = hd * SGU_GROUP_DIM
            mixed = _dot(ws[hd], v[r0:r0 + GMLP_CHUNK, c0:c0 + SGU_GROUP_DIM])
            mixed = mixed + sbias_ref[:, pl.ds(c0, SGU_GROUP_DIM)]
            tiles.append(u[r0:r0 + GMLP_CHUNK, c0:c0 + SGU_GROUP_DIM] * mixed)
        gated = jnp.concatenate(tiles, axis=1)
        vb_ref[pl.ds(r0, GMLP_CHUNK), pl.ds(CONV_CH, SGU_CH)] = _rms(gated, ons_ref[...]).astype(BF16)

    o_ref[...] += _dot(vb_ref[...], wout_ref[...])


def _mix(x, norm_g, w_in, w_out, conv_w, conv_b, cln_g, cln_b, sln_g, sln_b, sgu_w, sgu_bias,
         on_conv, on_sgu, *, seq):
    t, d = x.shape
    tm = MIX_TM
    resident = lambda a: pl.BlockSpec(a.shape, lambda i: (0,) * a.ndim, pipeline_mode=pl.Buffered(1))
    args = (norm_g, w_in, w_out, conv_w, conv_b, cln_g, cln_b, sln_g, sln_b, sgu_w, sgu_bias,
            on_conv, on_sgu)
    return pl.pallas_call(
        functools.partial(_mix_kernel, blocks_per_seq=seq // tm),
        out_shape=jax.ShapeDtypeStruct((t, d), F32),
        grid=(t // tm,),
        in_specs=[pl.BlockSpec(memory_space=pl.ANY)] + [resident(a) for a in args],
        out_specs=pl.BlockSpec((tm, d), lambda i: (i, 0)),
        scratch_shapes=[
            pltpu.VMEM((tm, d), BF16),
            pltpu.VMEM((HALO + tm, CONV_CH), F32),
            pltpu.VMEM((tm, CONV_CH), F32),
            pltpu.VMEM((tm, CONV_CH + SGU_CH), BF16),
            pltpu.SemaphoreType.DMA((tm // ROW_CHUNK,)),
        ],
        compiler_params=pltpu.CompilerParams(
            dimension_semantics=("arbitrary",), vmem_limit_bytes=VMEM_LIMIT),
        name="mix",
    )(x, *args)


def _kv_kernel(m_ref, g_ref, w_ref, o_ref, h_ref):
    @pl.when(pl.program_id(0) == 0)
    def _():
        h_ref[...] = _rms(m_ref[...], g_ref[...]).astype(BF16)
    o_ref[...] = _dot(h_ref[...], w_ref[...].astype(BF16)).astype(BF16)


def _kv(mem, norm_g, w_kv):
    m, d = mem.shape
    n = w_kv.shape[1]
    return pl.pallas_call(
        _kv_kernel,
        out_shape=jax.ShapeDtypeStruct((m, n), BF16),
        grid=(n // KV_TN,),
        in_specs=[
            pl.BlockSpec((m, d), lambda j: (0, 0)),
            pl.BlockSpec((1, d), lambda j: (0, 0)),
            pl.BlockSpec((d, KV_TN), lambda j: (0, j)),
        ],
        out_specs=pl.BlockSpec((m, KV_TN), lambda j: (0, j)),
        scratch_shapes=[pltpu.VMEM((m, d), BF16)],
        compiler_params=pltpu.CompilerParams(
            dimension_semantics=("arbitrary",), vmem_limit_bytes=VMEM_LIMIT),
        name="kv",
    )(mem, norm_g, w_kv)


def _attn_kernel(x_ref, g_ref, wq_ref, k_ref, v_ref, wo_ref, o_ref, h_ref):
    j = pl.program_id(1)
    tm = x_ref.shape[0]

    @pl.when(j == 0)
    def _():
        def norm_rows(rows):
            x = x_ref[rows, :]
            h_ref[rows, :] = _rms(x, g_ref[...]).astype(BF16)
            o_ref[rows, :] = x
        _for_row_chunks(tm, 128, norm_rows)

    q = _dot(h_ref[...], wq_ref[...]).astype(BF16)
    s = lax.dot_general(q, k_ref[0], (((1,), (1,)), ((), ())), preferred_element_type=F32)
    s = s * (XA_HEAD_DIM ** -0.5)
    e = jnp.exp(s - jnp.max(s, axis=-1, keepdims=True))
    p = (e / jnp.sum(e, axis=-1, keepdims=True)).astype(BF16)
    oh = _dot(p, v_ref[0]).astype(BF16)
    o_ref[...] += _dot(oh, wo_ref[...])


def _attn(x, norm_g, w_q, kv, w_o, *, seq):
    t, d = x.shape
    tm = ATT_TM
    bps = seq // tm
    n_mem = kv.shape[1]
    hd = XA_HEAD_DIM
    return pl.pallas_call(
        _attn_kernel,
        out_shape=jax.ShapeDtypeStruct((t, d), F32),
        grid=(t // tm, XA_HEADS),
        in_specs=[
            pl.BlockSpec((tm, d), lambda i, j: (i, 0)),
            pl.BlockSpec((1, d), lambda i, j: (0, 0)),
            pl.BlockSpec((d, hd), lambda i, j: (0, j)),
            pl.BlockSpec((1, n_mem, hd), lambda i, j: (i // bps, 0, j)),
            pl.BlockSpec((1, n_mem, hd), lambda i, j: (i // bps, 0, j + XA_HEADS)),
            pl.BlockSpec((hd, d), lambda i, j: (j, 0)),
        ],
        out_specs=pl.BlockSpec((tm, d), lambda i, j: (i, 0)),
        scratch_shapes=[pltpu.VMEM((tm, d), BF16)],
        compiler_params=pltpu.CompilerParams(
            dimension_semantics=("arbitrary", "arbitrary"), vmem_limit_bytes=VMEM_LIMIT),
        name="attn",
    )(x, norm_g, w_q, kv, kv, w_o)


def kernel(x, mem, ffn1_norm, ffn1_w_in, ffn1_w_out, mix_norm, w_mix_in, conv_w, conv_b, conv_ln_g, conv_ln_b, sgu_ln_g, sgu_ln_b, sgu_w, sgu_b, out_norm_conv, out_norm_sgu, w_mix_out, xattn_norm, mem_norm, w_q, w_kv, w_o, ffn2_norm, ffn2_w_in, ffn2_w_out, final_norm):
    b, s, d = x.shape
    n_mem = mem.shape[1]
    depth = ffn1_norm.shape[0]
    row = lambda v: v.reshape(1, -1)
    ones = jnp.ones((1, d), F32)

    xs = x.reshape(b * s, d)
    mem2 = mem.reshape(b * n_mem, d)
    for l in range(depth):
        last = l == depth - 1
        xs = _ffn(xs, row(ffn1_norm[l]), ffn1_w_in[l], ffn1_w_out[l], ones, final_norm=False)
        sgu_bias = jnp.repeat(jnp.transpose(sgu_b[l]), SGU_GROUP_DIM, axis=1)
        xs = _mix(xs, row(mix_norm[l]), w_mix_in[l].astype(BF16), w_mix_out[l].astype(BF16),
                  conv_w[l], row(conv_b[l]), row(conv_ln_g[l]), row(conv_ln_b[l]),
                  row(sgu_ln_g[l]), row(sgu_ln_b[l]), sgu_w[l], sgu_bias,
                  row(out_norm_conv[l]), row(out_norm_sgu[l]), seq=s)
        kv = _kv(mem2, row(mem_norm[l]), w_kv[l]).reshape(b, n_mem, 2 * d)
        xs = _attn(xs, row(xattn_norm[l]), w_q[l].astype(BF16), kv, w_o[l].astype(BF16), seq=s)
        xs = _ffn(xs, row(ffn2_norm[l]), ffn2_w_in[l], ffn2_w_out[l],
                  row(final_norm) if last else ones, final_norm=last)
    return xs.reshape(b, s, d)
```

```python
import functools

import jax
import jax.numpy as jnp
from jax import lax
from jax.experimental import pallas as pl
from jax.experimental.pallas import tpu as pltpu

D_MODEL = 2048
D_FF = 5632
CONV_CH = 1024
CONV_WIDTH = 31
SGU_CH = 1024
SGU_GROUPS = 8
SGU_GROUP_DIM = SGU_CH // SGU_GROUPS
GMLP_CHUNK = 128
CHUNK = 64
XA_HEADS = 4
XA_HEAD_DIM = D_MODEL // XA_HEADS
MACARON_SCALE = 0.5
RMS_EPS = 1e-6
LN_EPS = 1e-5

F32 = jnp.float32
BF16 = jnp.bfloat16

SUBLANES = 8
HALO = 32
VMEM_LIMIT = 62 * 1024 * 1024

FFN_TM, FFN_TF = 1024, 512
MIX_TM = 512
ATT_TM = 512
KV_TN = 512
CONV_ROWS = 64
ROW_CHUNK = 128


def _rms(x, g):
    ms = jnp.mean(x * x, axis=-1, keepdims=True)
    return x * lax.rsqrt(ms + RMS_EPS) * g


def _layer_norm(x, g, b):
    mu = jnp.mean(x, axis=-1, keepdims=True)
    xc = x - mu
    var = jnp.mean(xc * xc, axis=-1, keepdims=True)
    return xc * lax.rsqrt(var + LN_EPS) * g + b


def _silu(x):
    return x * jax.nn.sigmoid(x)


def _gelu(x):
    return 0.5 * x * (1.0 + lax.erf(x * (0.5 ** 0.5)))


def _dot(a, b):
    return jnp.dot(a, b, preferred_element_type=F32)


def _for_row_chunks(n_rows, chunk, body):
    def step(r, carry):
        body(pl.ds(pl.multiple_of(r * chunk, chunk), chunk))
        return carry
    lax.fori_loop(0, n_rows // chunk, step, 0)


def _resident(a):
    return pl.BlockSpec(a.shape, lambda *_: (0,) * a.ndim, pipeline_mode=pl.Buffered(1))


def _ffn_kernel(x_hbm, g_ref, wg_ref, wu_ref, wo_ref, fin_ref, o_ref, h_ref, xbuf, xsem, *, final_norm):
    i = pl.program_id(0)
    j = pl.program_id(1)
    tm = o_ref.shape[0]

    def x_copy(block):
        rows = pl.ds(pl.multiple_of(block * tm, tm), tm)
        return pltpu.make_async_copy(x_hbm.at[rows], xbuf, xsem.at[0])

    @pl.when((i == 0) & (j == 0))
    def _():
        x_copy(0).start()

    @pl.when(j == 0)
    def _():
        x_copy(i).wait()

        def norm_rows(rows):
            x = xbuf[rows, :]
            h_ref[rows, :] = _rms(x, g_ref[...]).astype(BF16)
            o_ref[rows, :] = x
        _for_row_chunks(tm, ROW_CHUNK, norm_rows)

    @pl.when((j == 1) & (i + 1 < pl.num_programs(0)))
    def _():
        x_copy(i + 1).start()

    h = h_ref[...]
    gate = _dot(h, wg_ref[...].astype(BF16))
    up = _dot(h, wu_ref[...].astype(BF16))
    a = (MACARON_SCALE * _silu(gate) * up).astype(BF16)
    o_ref[...] += _dot(a, wo_ref[...].astype(BF16))

    if final_norm:
        @pl.when(j == pl.num_programs(1) - 1)
        def _():
            def norm_rows(rows):
                o_ref[rows, :] = _rms(o_ref[rows, :], fin_ref[...])
            _for_row_chunks(tm, ROW_CHUNK, norm_rows)


def _ffn(x, norm_g, w_in, w_out, fin_g, *, final_norm):
    t, d = x.shape
    f = w_out.shape[0]
    tm, tf = FFN_TM, FFN_TF
    nf = f // tf
    assert nf >= 2, "the next row block is prefetched at hidden tile 1"
    return pl.pallas_call(
        functools.partial(_ffn_kernel, final_norm=final_norm),
        out_shape=jax.ShapeDtypeStruct((t, d), F32),
        grid=(t // tm, nf),
        in_specs=[
            pl.BlockSpec(memory_space=pl.ANY),
            pl.BlockSpec((1, d), lambda i, j: (0, 0)),
            pl.BlockSpec((d, tf), lambda i, j: (0, j)),
            pl.BlockSpec((d, tf), lambda i, j: (0, j + nf)),
            pl.BlockSpec((tf, d), lambda i, j: (j, 0)),
            pl.BlockSpec((1, d), lambda i, j: (0, 0)),
        ],
        out_specs=pl.BlockSpec((tm, d), lambda i, j: (i, 0)),
        scratch_shapes=[
            pltpu.VMEM((tm, d), BF16),
            pltpu.VMEM((tm, d), F32),
            pltpu.SemaphoreType.DMA((1,)),
        ],
        compiler_params=pltpu.CompilerParams(
            dimension_semantics=("arbitrary", "arbitrary"), vmem_limit_bytes=VMEM_LIMIT),
        name="ffn_final" if final_norm else "ffn",
    )(x, norm_g, w_in, w_in, w_out, fin_g)


def _conv_rows(ext_ref, w_ref, y_ref, rows):
    n = rows.size
    first = HALO - (CONV_WIDTH - 1)
    for c in range(CONV_CH // 128):
        lanes = pl.ds(c * 128, 128)
        acc = jnp.zeros((n, 128), F32)
        for r in range(SUBLANES):
            taps = [m for m in range(first, first + CONV_WIDTH) if m % SUBLANES == r]
            span = n if r == 0 else n + SUBLANES
            z = jnp.zeros((span, 128), F32)
            for m in taps:
                seg = ext_ref[pl.ds(rows.start + (m - r), span), lanes]
                z = z + w_ref[pl.ds(m - first, 1), lanes] * seg
            if r:
                z = pltpu.roll(z, span - r, 0)[:n]
            acc = acc + z
        y_ref[rows, lanes] = acc


def _mix_kernel(x_ref, g_ref, win_ref, wout_ref, convw_ref, convb_ref, clng_ref, clnb_ref,
                slng_ref, slnb_ref, sw_ref, sbias_ref, onc_ref, ons_ref,
                o_ref, ext_ref, y_ref, vb_ref, *, blocks_per_seq):
    i = pl.program_id(0)
    tm = o_ref.shape[0]

    @pl.when(i % blocks_per_seq == 0)
    def _():
        ext_ref[pl.ds(0, HALO), :] = jnp.zeros((HALO, CONV_CH), F32)

    h = _rms(x_ref[...], g_ref[...]).astype(BF16)
    p = _dot(h, win_ref[:, pl.ds(0, 2 * CONV_CH)])
    ext_ref[pl.ds(HALO, tm), :] = p[:, :CONV_CH] * jax.nn.sigmoid(p[:, CONV_CH:])
    q = _dot(h, win_ref[:, pl.ds(2 * CONV_CH, 2 * SGU_CH)])

    for r in range(tm // CONV_ROWS):
        _conv_rows(ext_ref, convw_ref, y_ref, pl.ds(r * CONV_ROWS, CONV_ROWS))
    ext_ref[pl.ds(0, HALO), :] = ext_ref[pl.ds(tm, HALO), :]

    y = y_ref[...] + convb_ref[...]
    y = _silu(_layer_norm(y, clng_ref[...], clnb_ref[...]))
    vb_ref[:, pl.ds(0, CONV_CH)] = _rms(y, onc_ref[...]).astype(BF16)

    u = _gelu(q[:, :SGU_CH])
    v = _layer_norm(_gelu(q[:, SGU_CH:]), slng_ref[...], slnb_ref[...]).astype(BF16)
    blk = lax.broadcasted_iota(jnp.int32, (GMLP_CHUNK, GMLP_CHUNK), 0) // CHUNK
    blk_t = lax.broadcasted_iota(jnp.int32, (GMLP_CHUNK, GMLP_CHUNK), 1) // CHUNK
    mask = blk_t <= blk
    ws = [jnp.where(mask, sw_ref[hd], 0.0).astype(BF16) for hd in range(SGU_GROUPS)]
    for n in range(tm // GMLP_CHUNK):
        r0 = n * GMLP_CHUNK
        tiles = []
        for hd in range(SGU_GROUPS):
            c0 = hd * SGU_GROUP_DIM
            mixed = _dot(ws[hd], v[r0:r0 + GMLP_CHUNK, c0:c0 + SGU_GROUP_DIM])
            mixed = mixed + sbias_ref[:, pl.ds(c0, SGU_GROUP_DIM)]
            tiles.append(u[r0:r0 + GMLP_CHUNK, c0:c0 + SGU_GROUP_DIM] * mixed)
        gated = jnp.concatenate(tiles, axis=1)
        vb_ref[pl.ds(r0, GMLP_CHUNK), pl.ds(CONV_CH, SGU_CH)] = _rms(gated, ons_ref[...]).astype(BF16)

    o_ref[...] = x_ref[...] + _dot(vb_ref[...], wout_ref[...])


def _mix(x, norm_g, w_in, w_out, conv_w, conv_b, cln_g, cln_b, sln_g, sln_b, sgu_w, sgu_bias,
         on_conv, on_sgu, *, seq):
    t, d = x.shape
    tm = MIX_TM
    args = (norm_g, w_in, w_out, conv_w, conv_b, cln_g, cln_b, sln_g, sln_b, sgu_w, sgu_bias,
            on_conv, on_sgu)
    return pl.pallas_call(
        functools.partial(_mix_kernel, blocks_per_seq=seq // tm),
        out_shape=jax.ShapeDtypeStruct((t, d), F32),
        grid=(t // tm,),
        in_specs=[pl.BlockSpec((tm, d), lambda i: (i, 0))] + [_resident(a) for a in args],
        out_specs=pl.BlockSpec((tm, d), lambda i: (i, 0)),
        scratch_shapes=[
            pltpu.VMEM((HALO + tm, CONV_CH), F32),
            pltpu.VMEM((tm, CONV_CH), F32),
            pltpu.VMEM((tm, CONV_CH + SGU_CH), BF16),
        ],
        compiler_params=pltpu.CompilerParams(
            dimension_semantics=("arbitrary",), vmem_limit_bytes=VMEM_LIMIT),
        name="mix",
    )(x, *args)


def _kv_kernel(m_ref, g_ref, w_ref, o_ref, h_ref):
    @pl.when(pl.program_id(0) == 0)
    def _():
        h_ref[...] = _rms(m_ref[...], g_ref[...]).astype(BF16)
    o_ref[...] = _dot(h_ref[...], w_ref[...].astype(BF16)).astype(BF16)


def _kv(mem, norm_g, w_kv):
    m, d = mem.shape
    n = w_kv.shape[1]
    return pl.pallas_call(
        _kv_kernel,
        out_shape=jax.ShapeDtypeStruct((m, n), BF16),
        grid=(n // KV_TN,),
        in_specs=[
            pl.BlockSpec((m, d), lambda j: (0, 0)),
            pl.BlockSpec((1, d), lambda j: (0, 0)),
            pl.BlockSpec((d, KV_TN), lambda j: (0, j)),
        ],
        out_specs=pl.BlockSpec((m, KV_TN), lambda j: (0, j)),
        scratch_shapes=[pltpu.VMEM((m, d), BF16)],
        compiler_params=pltpu.CompilerParams(
            dimension_semantics=("arbitrary",), vmem_limit_bytes=VMEM_LIMIT),
        name="kv",
    )(mem, norm_g, w_kv)


def _attn_kernel(x_ref, g_ref, wq_ref, k_ref, v_ref, wo_ref, o_ref, oh_ref):
    h = _rms(x_ref[...], g_ref[...]).astype(BF16)
    q = _dot(h, wq_ref[...]).astype(BF16)
    for hd in range(XA_HEADS):
        cols = pl.ds(hd * XA_HEAD_DIM, XA_HEAD_DIM)
        qh = q[:, hd * XA_HEAD_DIM:(hd + 1) * XA_HEAD_DIM]
        s = lax.dot_general(qh, k_ref[0, :, cols], (((1,), (1,)), ((), ())), preferred_element_type=F32)
        s = s * (XA_HEAD_DIM ** -0.5)
        e = jnp.exp(s - jnp.max(s, axis=-1, keepdims=True))
        p = (e / jnp.sum(e, axis=-1, keepdims=True)).astype(BF16)
        oh_ref[:, cols] = _dot(p, v_ref[0, :, cols]).astype(BF16)
    o_ref[...] = x_ref[...] + _dot(oh_ref[...], wo_ref[...])


def _attn(x, norm_g, w_q, kv, w_o, *, seq):
    t, d = x.shape
    tm = ATT_TM
    bps = seq // tm
    n_mem = kv.shape[1]
    return pl.pallas_call(
        _attn_kernel,
        out_shape=jax.ShapeDtypeStruct((t, d), F32),
        grid=(t // tm,),
        in_specs=[
            pl.BlockSpec((tm, d), lambda i: (i, 0)),
            _resident(norm_g),
            _resident(w_q),
            pl.BlockSpec((1, n_mem, d), lambda i: (i // bps, 0, 0)),
            pl.BlockSpec((1, n_mem, d), lambda i: (i // bps, 0, 1)),
            _resident(w_o),
        ],
        out_specs=pl.BlockSpec((tm, d), lambda i: (i, 0)),
        scratch_shapes=[pltpu.VMEM((tm, d), BF16)],
        compiler_params=pltpu.CompilerParams(
            dimension_semantics=("arbitrary",), vmem_limit_bytes=VMEM_LIMIT),
        name="attn",
    )(x, norm_g, w_q, kv, kv, w_o)


def kernel(x, mem, ffn1_norm, ffn1_w_in, ffn1_w_out, mix_norm, w_mix_in, conv_w, conv_b, conv_ln_g, conv_ln_b, sgu_ln_g, sgu_ln_b, sgu_w, sgu_b, out_norm_conv, out_norm_sgu, w_mix_out, xattn_norm, mem_norm, w_q, w_kv, w_o, ffn2_norm, ffn2_w_in, ffn2_w_out, final_norm):
    b, s, d = x.shape
    n_mem = mem.shape[1]
    depth = ffn1_norm.shape[0]
    row = lambda v: v.reshape(1, -1)
    ones = jnp.ones((1, d), F32)

    xs = x.reshape(b * s, d)
    mem2 = mem.reshape(b * n_mem, d)
    for l in range(depth):
        last = l == depth - 1
        xs = _ffn(xs, row(ffn1_norm[l]), ffn1_w_in[l], ffn1_w_out[l], ones, final_norm=False)
        sgu_bias = jnp.repeat(jnp.transpose(sgu_b[l]), SGU_GROUP_DIM, axis=1)
        xs = _mix(xs, row(mix_norm[l]), w_mix_in[l].astype(BF16), w_mix_out[l].astype(BF16),
                  conv_w[l], row(conv_b[l]), row(conv_ln_g[l]), row(conv_ln_b[l]),
                  row(sgu_ln_g[l]), row(sgu_ln_b[l]), sgu_w[l], sgu_bias,
                  row(out_norm_conv[l]), row(out_norm_sgu[l]), seq=s)
        kv = _kv(mem2, row(mem_norm[l]), w_kv[l]).reshape(b, n_mem, 2 * d)
        xs = _attn(xs, row(xattn_norm[l]), w_q[l].astype(BF16), kv, w_o[l].astype(BF16), seq=s)
        xs = _ffn(xs, row(ffn2_norm[l]), ffn2_w_in[l], ffn2_w_out[l],
                  row(final_norm) if last else ones, final_norm=last)
    return xs.reshape(b, s, d)
```

```python
import functools

import jax
import jax.numpy as jnp
from jax import lax
from jax.experimental import pallas as pl
from jax.experimental.pallas import tpu as pltpu

D_MODEL = 2048
D_FF = 5632
CONV_CH = 1024
CONV_WIDTH = 31
SGU_CH = 1024
SGU_GROUPS = 8
SGU_GROUP_DIM = SGU_CH // SGU_GROUPS
GMLP_CHUNK = 128
CHUNK = 64
XA_HEADS = 4
XA_HEAD_DIM = D_MODEL // XA_HEADS
MACARON_SCALE = 0.5
RMS_EPS = 1e-6
LN_EPS = 1e-5

F32 = jnp.float32
BF16 = jnp.bfloat16

LANES = 128
SUBLANES = 8
HALO = 32
VMEM_LIMIT = 62 * 1024 * 1024

FFN_TM, FFN_TF = 1024, 512
MIX_TM = 512
ATT_TM = 512
KV_TN = 512
CONV_ROWS = 64
ROW_CHUNK = 128
CAST_ROWS = 32


def _rms(x, g):
    ms = jnp.mean(x * x, axis=-1, keepdims=True)
    return x * lax.rsqrt(ms + RMS_EPS) * g


def _layer_norm(x, g, b):
    mu = jnp.mean(x, axis=-1, keepdims=True)
    xc = x - mu
    var = jnp.mean(xc * xc, axis=-1, keepdims=True)
    return xc * lax.rsqrt(var + LN_EPS) * g + b


def _silu(x):
    return x * jax.nn.sigmoid(x)


def _gelu(x):
    return 0.5 * x * (1.0 + lax.erf(x * (0.5 ** 0.5)))


def _dot(a, b):
    return jnp.dot(a, b, preferred_element_type=F32)


def _for_row_chunks(n_rows, chunk, body):
    def step(r, carry):
        body(pl.ds(pl.multiple_of(r * chunk, chunk), chunk))
        return carry
    lax.fori_loop(0, n_rows // chunk, step, 0)


def _resident(a):
    return pl.BlockSpec(a.shape, lambda *_: (0,) * a.ndim, pipeline_mode=pl.Buffered(1))


def _ffn_kernel(x_hbm, g_ref, wg_ref, wu_ref, wo_ref, fin_ref, *rest, final_norm, n_side):
    side_in, o_ref, side_out = rest[:n_side], rest[n_side], rest[n_side + 1:2 * n_side + 1]
    h_ref, xbuf, xsem = rest[2 * n_side + 1:]
    i = pl.program_id(0)
    j = pl.program_id(1)
    tm = o_ref.shape[0]

    def x_copy(block):
        rows = pl.ds(pl.multiple_of(block * tm, tm), tm)
        return pltpu.make_async_copy(x_hbm.at[rows], xbuf, xsem.at[0])

    @pl.when((i == 0) & (j == 0))
    def _():
        x_copy(0).start()

    def hidden_tile(h):
        gate = _dot(h, wg_ref[...].astype(BF16))
        up = _dot(h, wu_ref[...].astype(BF16))
        a = (MACARON_SCALE * _silu(gate) * up).astype(BF16)
        out = _dot(a, wo_ref[...].astype(BF16))
        for src, dst in zip(side_in, side_out):
            dst[...] = src[...].astype(BF16)
        return out

    @pl.when(j == 0)
    def _():
        x_copy(i).wait()
        x = xbuf[...]
        h = _rms(x, g_ref[...]).astype(BF16)
        h_ref[...] = h
        o_ref[...] = x + hidden_tile(h)

    @pl.when((j == 1) & (i + 1 < pl.num_programs(0)))
    def _():
        x_copy(i + 1).start()

    @pl.when(j > 0)
    def _():
        o_ref[...] += hidden_tile(h_ref[...])

    if final_norm:
        @pl.when(j == pl.num_programs(1) - 1)
        def _():
            def norm_rows(rows):
                o_ref[rows, :] = _rms(o_ref[rows, :], fin_ref[...])
            _for_row_chunks(tm, ROW_CHUNK, norm_rows)


def _ffn(x, norm_g, w_in, w_out, fin_g, *, final_norm, cast_to_bf16=()):
    t, d = x.shape
    f = w_out.shape[0]
    tm, tf = FFN_TM, FFN_TF
    nf = f // tf
    n_steps = (t // tm) * nf
    assert nf >= 2, "the next row block is prefetched at hidden tile 1"
    side = list(cast_to_bf16)

    def slab_spec(w):
        n_slabs = w.shape[0] // CAST_ROWS
        assert w.shape[0] % CAST_ROWS == 0 and n_slabs <= n_steps
        return pl.BlockSpec((CAST_ROWS, w.shape[1]), lambda i, j: (jnp.minimum(i * nf + j, n_slabs - 1), 0))

    side_specs = [slab_spec(w) for w in side]
    outs = pl.pallas_call(
        functools.partial(_ffn_kernel, final_norm=final_norm, n_side=len(side)),
        out_shape=[jax.ShapeDtypeStruct((t, d), F32)] + [jax.ShapeDtypeStruct(w.shape, BF16) for w in side],
        grid=(t // tm, nf),
        in_specs=[
            pl.BlockSpec(memory_space=pl.ANY),
            pl.BlockSpec((1, d), lambda i, j: (0, 0)),
            pl.BlockSpec((d, tf), lambda i, j: (0, j)),
            pl.BlockSpec((d, tf), lambda i, j: (0, j + nf)),
            pl.BlockSpec((tf, d), lambda i, j: (j, 0)),
            pl.BlockSpec((1, d), lambda i, j: (0, 0)),
        ] + side_specs,
        out_specs=[pl.BlockSpec((tm, d), lambda i, j: (i, 0))] + side_specs,
        scratch_shapes=[
            pltpu.VMEM((tm, d), BF16),
            pltpu.VMEM((tm, d), F32),
            pltpu.SemaphoreType.DMA((1,)),
        ],
        compiler_params=pltpu.CompilerParams(
            dimension_semantics=("arbitrary", "arbitrary"), vmem_limit_bytes=VMEM_LIMIT),
        name="ffn_final" if final_norm else "ffn",
    )(x, norm_g, w_in, w_in, w_out, fin_g, *side)
    return outs[0], outs[1:]


def _conv_unit(ext_ref, w_ref, y_ref, row0, c):
    n = CONV_ROWS
    first = HALO - (CONV_WIDTH - 1)
    lanes = pl.ds(c * LANES, LANES)
    acc = jnp.zeros((n, LANES), F32)
    for r in range(SUBLANES):
        taps = [m for m in range(first, first + CONV_WIDTH) if m % SUBLANES == r]
        span = n if r == 0 else n + SUBLANES
        z = jnp.zeros((span, LANES), F32)
        for m in taps:
            seg = ext_ref[pl.ds(row0 + (m - r), span), lanes]
            z = z + w_ref[pl.ds(m - first, 1), lanes] * seg
        if r:
            z = pltpu.roll(z, span - r, 0)[:n]
        acc = acc + z
    y_ref[pl.ds(row0, n), lanes] = acc


def _mix_kernel(x_ref, g_ref, win_ref, wout_ref, convw_ref, convb_ref, clng_ref, clnb_ref,
                slng_ref, slnb_ref, sw_ref, sbias_ref, onc_ref, ons_ref,
                o_ref, ext_ref, y_ref, vb_ref, *, blocks_per_seq):
    i = pl.program_id(0)
    tm = o_ref.shape[0]

    @pl.when(i % blocks_per_seq == 0)
    def _():
        ext_ref[pl.ds(0, HALO), :] = jnp.zeros((HALO, CONV_CH), F32)

    h = _rms(x_ref[...], g_ref[...]).astype(BF16)
    p = _dot(h, win_ref[:, pl.ds(0, 2 * CONV_CH)])
    ext_ref[pl.ds(HALO, tm), :] = p[:, :CONV_CH] * jax.nn.sigmoid(p[:, CONV_CH:])
    q = _dot(h, win_ref[:, pl.ds(2 * CONV_CH, 2 * SGU_CH)])

    for r in range(tm // CONV_ROWS):
        for c in range(CONV_CH // LANES):
            _conv_unit(ext_ref, convw_ref, y_ref, r * CONV_ROWS, c)
    ext_ref[pl.ds(0, HALO), :] = ext_ref[pl.ds(tm, HALO), :]

    y = y_ref[...] + convb_ref[...]
    y = _silu(_layer_norm(y, clng_ref[...], clnb_ref[...]))
    vb_ref[:, pl.ds(0, CONV_CH)] = _rms(y, onc_ref[...]).astype(BF16)

    u = _gelu(q[:, :SGU_CH])
    v = _layer_norm(_gelu(q[:, SGU_CH:]), slng_ref[...], slnb_ref[...]).astype(BF16)
    blk = lax.broadcasted_iota(jnp.int32, (GMLP_CHUNK, GMLP_CHUNK), 0) // CHUNK
    blk_t = lax.broadcasted_iota(jnp.int32, (GMLP_CHUNK, GMLP_CHUNK), 1) // CHUNK
    mask = blk_t <= blk
    ws = [jnp.where(mask, sw_ref[hd], 0.0).astype(BF16) for hd in range(SGU_GROUPS)]
    for n in range(tm // GMLP_CHUNK):
        r0 = n * GMLP_CHUNK
        tiles = []
        for hd in range(SGU_GROUPS):
            c0 = hd * SGU_GROUP_DIM
            mixed = _dot(ws[hd], v[r0:r0 + GMLP_CHUNK, c0:c0 + SGU_GROUP_DIM])
            mixed = mixed + sbias_ref[:, pl.ds(c0, SGU_GROUP_DIM)]
            tiles.append(u[r0:r0 + GMLP_CHUNK, c0:c0 + SGU_GROUP_DIM] * mixed)
        gated = jnp.concatenate(tiles, axis=1)
        vb_ref[pl.ds(r0, GMLP_CHUNK), pl.ds(CONV_CH, SGU_CH)] = _rms(gated, ons_ref[...]).astype(BF16)

    o_ref[...] = x_ref[...] + _dot(vb_ref[...], wout_ref[...])


def _mix(x, norm_g, w_in, w_out, conv_w, conv_b, cln_g, cln_b, sln_g, sln_b, sgu_w, sgu_bias,
         on_conv, on_sgu, *, seq):
    t, d = x.shape
    tm = MIX_TM
    args = (norm_g, w_in, w_out, conv_w, conv_b, cln_g, cln_b, sln_g, sln_b, sgu_w, sgu_bias,
            on_conv, on_sgu)
    return pl.pallas_call(
        functools.partial(_mix_kernel, blocks_per_seq=seq // tm),
        out_shape=jax.ShapeDtypeStruct((t, d), F32),
        grid=(t // tm,),
        in_specs=[pl.BlockSpec((tm, d), lambda i: (i, 0))] + [_resident(a) for a in args],
        out_specs=pl.BlockSpec((tm, d), lambda i: (i, 0)),
        scratch_shapes=[
            pltpu.VMEM((HALO + tm, CONV_CH), F32),
            pltpu.VMEM((tm, CONV_CH), F32),
            pltpu.VMEM((tm, CONV_CH + SGU_CH), BF16),
        ],
        compiler_params=pltpu.CompilerParams(
            dimension_semantics=("arbitrary",), vmem_limit_bytes=VMEM_LIMIT),
        name="mix",
    )(x, *args)


def _kv_kernel(m_ref, g_ref, w_ref, *rest, n_side):
    side_in, o_ref, side_out, h_ref = rest[:n_side], rest[n_side], rest[n_side + 1:2 * n_side + 1], rest[-1]

    @pl.when(pl.program_id(0) == 0)
    def _():
        h_ref[...] = _rms(m_ref[...], g_ref[...]).astype(BF16)
    o_ref[...] = _dot(h_ref[...], w_ref[...].astype(BF16)).astype(BF16)
    for src, dst in zip(side_in, side_out):
        dst[...] = src[...].astype(BF16)


def _kv(mem, norm_g, w_kv, cast_to_bf16=()):
    m, d = mem.shape
    n = w_kv.shape[1]
    n_steps = n // KV_TN
    side = list(cast_to_bf16)

    def slab_spec(w):
        assert w.shape[0] % n_steps == 0
        return pl.BlockSpec((w.shape[0] // n_steps, w.shape[1]), lambda j: (j, 0))

    side_specs = [slab_spec(w) for w in side]
    outs = pl.pallas_call(
        functools.partial(_kv_kernel, n_side=len(side)),
        out_shape=[jax.ShapeDtypeStruct((m, n), BF16)] + [jax.ShapeDtypeStruct(w.shape, BF16) for w in side],
        grid=(n_steps,),
        in_specs=[
            pl.BlockSpec((m, d), lambda j: (0, 0)),
            pl.BlockSpec((1, d), lambda j: (0, 0)),
            pl.BlockSpec((d, KV_TN), lambda j: (0, j)),
        ] + side_specs,
        out_specs=[pl.BlockSpec((m, KV_TN), lambda j: (0, j))] + side_specs,
        scratch_shapes=[pltpu.VMEM((m, d), BF16)],
        compiler_params=pltpu.CompilerParams(
            dimension_semantics=("arbitrary",), vmem_limit_bytes=VMEM_LIMIT),
        name="kv",
    )(mem, norm_g, w_kv, *side)
    return outs[0], outs[1:]


def _attn_kernel(x_ref, g_ref, wq_ref, k_ref, v_ref, wo_ref, o_ref, oh_ref):
    h = _rms(x_ref[...], g_ref[...]).astype(BF16)
    q = _dot(h, wq_ref[...]).astype(BF16)
    for hd in range(XA_HEADS):
        cols = pl.ds(hd * XA_HEAD_DIM, XA_HEAD_DIM)
        qh = q[:, hd * XA_HEAD_DIM:(hd + 1) * XA_HEAD_DIM]
        s = lax.dot_general(qh, k_ref[0, :, cols], (((1,), (1,)), ((), ())), preferred_element_type=F32)
        s = s * (XA_HEAD_DIM ** -0.5)
        e = jnp.exp(s - jnp.max(s, axis=-1, keepdims=True))
        p = (e / jnp.sum(e, axis=-1, keepdims=True)).astype(BF16)
        oh_ref[:, cols] = _dot(p, v_ref[0, :, cols]).astype(BF16)
    o_ref[...] = x_ref[...] + _dot(oh_ref[...], wo_ref[...])


def _attn(x, norm_g, w_q, kv, w_o, *, seq):
    t, d = x.shape
    tm = ATT_TM
    bps = seq // tm
    n_mem = kv.shape[1]
    return pl.pallas_call(
        _attn_kernel,
        out_shape=jax.ShapeDtypeStruct((t, d), F32),
        grid=(t // tm,),
        in_specs=[
            pl.BlockSpec((tm, d), lambda i: (i, 0)),
            _resident(norm_g),
            _resident(w_q),
            pl.BlockSpec((1, n_mem, d), lambda i: (i // bps, 0, 0)),
            pl.BlockSpec((1, n_mem, d), lambda i: (i // bps, 0, 1)),
            _resident(w_o),
        ],
        out_specs=pl.BlockSpec((tm, d), lambda i: (i, 0)),
        scratch_shapes=[pltpu.VMEM((tm, d), BF16)],
        compiler_params=pltpu.CompilerParams(
            dimension_semantics=("arbitrary",), vmem_limit_bytes=VMEM_LIMIT),
        name="attn",
    )(x, norm_g, w_q, kv, kv, w_o)


def kernel(x, mem, ffn1_norm, ffn1_w_in, ffn1_w_out, mix_norm, w_mix_in, conv_w, conv_b, conv_ln_g, conv_ln_b, sgu_ln_g, sgu_ln_b, sgu_w, sgu_b, out_norm_conv, out_norm_sgu, w_mix_out, xattn_norm, mem_norm, w_q, w_kv, w_o, ffn2_norm, ffn2_w_in, ffn2_w_out, final_norm):
    b, s, d = x.shape
    n_mem = mem.shape[1]
    depth = ffn1_norm.shape[0]
    row = lambda v: v.reshape(1, -1)
    ones = jnp.ones((1, d), F32)

    xs = x.reshape(b * s, d)
    mem2 = mem.reshape(b * n_mem, d)
    for l in range(depth):
        last = l == depth - 1
        xs, (w_mix_in_b, w_mix_out_b) = _ffn(
            xs, row(ffn1_norm[l]), ffn1_w_in[l], ffn1_w_out[l], ones, final_norm=False,
            cast_to_bf16=(w_mix_in[l], w_mix_out[l]))
        sgu_bias = jnp.repeat(jnp.transpose(sgu_b[l]), SGU_GROUP_DIM, axis=1)
        xs = _mix(xs, row(mix_norm[l]), w_mix_in_b, w_mix_out_b,
                  conv_w[l], row(conv_b[l]), row(conv_ln_g[l]), row(conv_ln_b[l]),
                  row(sgu_ln_g[l]), row(sgu_ln_b[l]), sgu_w[l], sgu_bias,
                  row(out_norm_conv[l]), row(out_norm_sgu[l]), seq=s)
        kv, (w_q_b, w_o_b) = _kv(mem2, row(mem_norm[l]), w_kv[l], cast_to_bf16=(w_q[l], w_o[l]))
        xs = _attn(xs, row(xattn_norm[l]), w_q_b, kv.reshape(b, n_mem, 2 * d), w_o_b, seq=s)
        xs, _ = _ffn(xs, row(ffn2_norm[l]), ffn2_w_in[l], ffn2_w_out[l],
                     row(final_norm) if last else ones, final_norm=last)
    return xs.reshape(b, s, d)
```

```python
import functools

import jax
import jax.numpy as jnp
from jax import lax
from jax.experimental import pallas as pl
from jax.experimental.pallas import tpu as pltpu

D_MODEL = 2048
D_FF = 5632
CONV_CH = 1024
CONV_WIDTH = 31
SGU_CH = 1024
SGU_GROUPS = 8
SGU_GROUP_DIM = SGU_CH // SGU_GROUPS
GMLP_CHUNK = 128
CHUNK = 64
XA_HEADS = 4
XA_HEAD_DIM = D_MODEL // XA_HEADS
MACARON_SCALE = 0.5
RMS_EPS = 1e-6
LN_EPS = 1e-5

F32 = jnp.float32
BF16 = jnp.bfloat16

LANES = 128
SUBLANES = 8
HALO = 32
VMEM_LIMIT = 62 * 1024 * 1024

FFN_TM, FFN_TF = 1024, 512
MIX_TM = 512
ATT_TM = 512
KV_TN = 512
CONV_ROWS = 64
EXT_PITCH = 2
ROW_CHUNK = 128
CAST_ROWS = 32


def _rms(x, g):
    ms = jnp.mean(x * x, axis=-1, keepdims=True)
    return x * lax.rsqrt(ms + RMS_EPS) * g


def _layer_norm(x, g, b):
    mu = jnp.mean(x, axis=-1, keepdims=True)
    xc = x - mu
    var = jnp.mean(xc * xc, axis=-1, keepdims=True)
    return xc * lax.rsqrt(var + LN_EPS) * g + b


def _silu(x):
    return x * jax.nn.sigmoid(x)


def _gelu(x):
    return 0.5 * x * (1.0 + lax.erf(x * (0.5 ** 0.5)))


def _dot(a, b):
    return jnp.dot(a, b, preferred_element_type=F32)


def _for_row_chunks(n_rows, chunk, body):
    def step(r, carry):
        body(pl.ds(pl.multiple_of(r * chunk, chunk), chunk))
        return carry
    lax.fori_loop(0, n_rows // chunk, step, 0)


def _resident(a):
    return pl.BlockSpec(a.shape, lambda *_: (0,) * a.ndim, pipeline_mode=pl.Buffered(1))


def _ffn_kernel(x_hbm, g_ref, wg_ref, wu_ref, wo_ref, fin_ref, *rest, final_norm, n_side):
    side_in, o_ref, side_out = rest[:n_side], rest[n_side], rest[n_side + 1:2 * n_side + 1]
    h_ref, xbuf, xsem = rest[2 * n_side + 1:]
    i = pl.program_id(0)
    j = pl.program_id(1)
    tm = o_ref.shape[0]

    def x_copy(block):
        rows = pl.ds(pl.multiple_of(block * tm, tm), tm)
        return pltpu.make_async_copy(x_hbm.at[rows], xbuf, xsem.at[0])

    @pl.when((i == 0) & (j == 0))
    def _():
        x_copy(0).start()

    def hidden_tile(h):
        gate = _dot(h, wg_ref[...].astype(BF16))
        up = _dot(h, wu_ref[...].astype(BF16))
        a = (MACARON_SCALE * _silu(gate) * up).astype(BF16)
        out = _dot(a, wo_ref[...].astype(BF16))
        for src, dst in zip(side_in, side_out):
            dst[...] = src[...].astype(BF16)
        return out

    @pl.when(j == 0)
    def _():
        x_copy(i).wait()
        x = xbuf[...]
        h = _rms(x, g_ref[...]).astype(BF16)
        h_ref[...] = h
        o_ref[...] = x + hidden_tile(h)

    @pl.when((j == 1) & (i + 1 < pl.num_programs(0)))
    def _():
        x_copy(i + 1).start()

    @pl.when(j > 0)
    def _():
        o_ref[...] += hidden_tile(h_ref[...])

    if final_norm:
        @pl.when(j == pl.num_programs(1) - 1)
        def _():
            def norm_rows(rows):
                o_ref[rows, :] = _rms(o_ref[rows, :], fin_ref[...])
            _for_row_chunks(tm, ROW_CHUNK, norm_rows)


def _ffn(x, norm_g, w_in, w_out, fin_g, *, final_norm, cast_to_bf16=()):
    t, d = x.shape
    f = w_out.shape[0]
    tm, tf = FFN_TM, FFN_TF
    nf = f // tf
    n_steps = (t // tm) * nf
    assert nf >= 2, "the next row block is prefetched at hidden tile 1"
    side = list(cast_to_bf16)

    def slab_spec(w):
        n_slabs = w.shape[0] // CAST_ROWS
        assert w.shape[0] % CAST_ROWS == 0 and n_slabs <= n_steps
        return pl.BlockSpec((CAST_ROWS, w.shape[1]), lambda i, j: (jnp.minimum(i * nf + j, n_slabs - 1), 0))

    side_specs = [slab_spec(w) for w in side]
    outs = pl.pallas_call(
        functools.partial(_ffn_kernel, final_norm=final_norm, n_side=len(side)),
        out_shape=[jax.ShapeDtypeStruct((t, d), F32)] + [jax.ShapeDtypeStruct(w.shape, BF16) for w in side],
        grid=(t // tm, nf),
        in_specs=[
            pl.BlockSpec(memory_space=pl.ANY),
            pl.BlockSpec((1, d), lambda i, j: (0, 0)),
            pl.BlockSpec((d, tf), lambda i, j: (0, j)),
            pl.BlockSpec((d, tf), lambda i, j: (0, j + nf)),
            pl.BlockSpec((tf, d), lambda i, j: (j, 0)),
            pl.BlockSpec((1, d), lambda i, j: (0, 0)),
        ] + side_specs,
        out_specs=[pl.BlockSpec((tm, d), lambda i, j: (i, 0))] + side_specs,
        scratch_shapes=[
            pltpu.VMEM((tm, d), BF16),
            pltpu.VMEM((tm, d), F32),
            pltpu.SemaphoreType.DMA((1,)),
        ],
        compiler_params=pltpu.CompilerParams(
            dimension_semantics=("arbitrary", "arbitrary"), vmem_limit_bytes=VMEM_LIMIT),
        name="ffn_final" if final_norm else "ffn",
    )(x, norm_g, w_in, w_in, w_out, fin_g, *side)
    return outs[0], outs[1:]


def _conv_unit(ext_ref, w_ref, y_ref, row0, c):
    n = CONV_ROWS
    first = HALO - (CONV_WIDTH - 1)
    lanes = pl.ds(c * LANES, LANES)
    acc = jnp.zeros((n, LANES), F32)
    for k in range(CONV_WIDTH):
        window = ext_ref[c, pl.ds(EXT_PITCH * (row0 + first + k), n, stride=EXT_PITCH), :]
        acc = acc + w_ref[pl.ds(k, 1), lanes] * window
    y_ref[pl.ds(row0, n), lanes] = acc


def _mix_kernel(x_ref, g_ref, win_ref, wout_ref, convw_ref, convb_ref, clng_ref, clnb_ref,
                slng_ref, slnb_ref, sw_ref, sbias_ref, onc_ref, ons_ref,
                o_ref, ext_ref, y_ref, vb_ref, *, blocks_per_seq):
    i = pl.program_id(0)
    tm = o_ref.shape[0]

    n_tiles = CONV_CH // LANES
    halo_rows = EXT_PITCH * HALO

    @pl.when(i % blocks_per_seq == 0)
    def _():
        ext_ref[:, pl.ds(0, halo_rows), :] = jnp.zeros((n_tiles, halo_rows, LANES), F32)

    h = _rms(x_ref[...], g_ref[...]).astype(BF16)
    p = _dot(h, win_ref[:, pl.ds(0, 2 * CONV_CH)])
    a = p[:, :CONV_CH] * jax.nn.sigmoid(p[:, CONV_CH:])
    for c in range(n_tiles):
        ext_ref[c, pl.ds(halo_rows, tm, stride=EXT_PITCH), :] = a[:, c * LANES:(c + 1) * LANES]
    q = _dot(h, win_ref[:, pl.ds(2 * CONV_CH, 2 * SGU_CH)])

    for r in range(tm // CONV_ROWS):
        for c in range(n_tiles):
            _conv_unit(ext_ref, convw_ref, y_ref, r * CONV_ROWS, c)
    for c in range(n_tiles):
        tail = ext_ref[c, pl.ds(EXT_PITCH * tm, HALO, stride=EXT_PITCH), :]
        ext_ref[c, pl.ds(0, HALO, stride=EXT_PITCH), :] = tail

    y = y_ref[...] + convb_ref[...]
    y = _silu(_layer_norm(y, clng_ref[...], clnb_ref[...]))
    vb_ref[:, pl.ds(0, CONV_CH)] = _rms(y, onc_ref[...]).astype(BF16)

    u = _gelu(q[:, :SGU_CH])
    v = _layer_norm(_gelu(q[:, SGU_CH:]), slng_ref[...], slnb_ref[...]).astype(BF16)
    blk = lax.broadcasted_iota(jnp.int32, (GMLP_CHUNK, GMLP_CHUNK), 0) // CHUNK
    blk_t = lax.broadcasted_iota(jnp.int32, (GMLP_CHUNK, GMLP_CHUNK), 1) // CHUNK
    mask = blk_t <= blk
    ws = [jnp.where(mask, sw_ref[hd], 0.0).astype(BF16) for hd in range(SGU_GROUPS)]
    for n in range(tm // GMLP_CHUNK):
        r0 = n * GMLP_CHUNK
        tiles = []
        for hd in range(SGU_GROUPS):
            c0 = hd * SGU_GROUP_DIM
            mixed = _dot(ws[hd], v[r0:r0 + GMLP_CHUNK, c0:c0 + SGU_GROUP_DIM])
            mixed = mixed + sbias_ref[:, pl.ds(c0, SGU_GROUP_DIM)]
            tiles.append(u[r0:r0 + GMLP_CHUNK, c0:c0 + SGU_GROUP_DIM] * mixed)
        gated = jnp.concatenate(tiles, axis=1)
        vb_ref[pl.ds(r0, GMLP_CHUNK), pl.ds(CONV_CH, SGU_CH)] = _rms(gated, ons_ref[...]).astype(BF16)

    o_ref[...] = x_ref[...] + _dot(vb_ref[...], wout_ref[...])


def _mix(x, norm_g, w_in, w_out, conv_w, conv_b, cln_g, cln_b, sln_g, sln_b, sgu_w, sgu_bias,
         on_conv, on_sgu, *, seq):
    t, d = x.shape
    tm = MIX_TM
    args = (norm_g, w_in, w_out, conv_w, conv_b, cln_g, cln_b, sln_g, sln_b, sgu_w, sgu_bias,
            on_conv, on_sgu)
    return pl.pallas_call(
        functools.partial(_mix_kernel, blocks_per_seq=seq // tm),
        out_shape=jax.ShapeDtypeStruct((t, d), F32),
        grid=(t // tm,),
        in_specs=[pl.BlockSpec((tm, d), lambda i: (i, 0))] + [_resident(a) for a in args],
        out_specs=pl.BlockSpec((tm, d), lambda i: (i, 0)),
        scratch_shapes=[
            pltpu.VMEM((CONV_CH // LANES, EXT_PITCH * (HALO + tm), LANES), F32),
            pltpu.VMEM((tm, CONV_CH), F32),
            pltpu.VMEM((tm, CONV_CH + SGU_CH), BF16),
        ],
        compiler_params=pltpu.CompilerParams(
            dimension_semantics=("arbitrary",), vmem_limit_bytes=VMEM_LIMIT),
        name="mix",
    )(x, *args)


def _kv_kernel(m_ref, g_ref, w_ref, *rest, n_side):
    side_in, o_ref, side_out, h_ref = rest[:n_side], rest[n_side], rest[n_side + 1:2 * n_side + 1], rest[-1]

    @pl.when(pl.program_id(0) == 0)
    def _():
        h_ref[...] = _rms(m_ref[...], g_ref[...]).astype(BF16)
    o_ref[...] = _dot(h_ref[...], w_ref[...].astype(BF16)).astype(BF16)
    for src, dst in zip(side_in, side_out):
        dst[...] = src[...].astype(BF16)


def _kv(mem, norm_g, w_kv, cast_to_bf16=()):
    m, d = mem.shape
    n = w_kv.shape[1]
    n_steps = n // KV_TN
    side = list(cast_to_bf16)

    def slab_spec(w):
        assert w.shape[0] % n_steps == 0
        return pl.BlockSpec((w.shape[0] // n_steps, w.shape[1]), lambda j: (j, 0))

    side_specs = [slab_spec(w) for w in side]
    outs = pl.pallas_call(
        functools.partial(_kv_kernel, n_side=len(side)),
        out_shape=[jax.ShapeDtypeStruct((m, n), BF16)] + [jax.ShapeDtypeStruct(w.shape, BF16) for w in side],
        grid=(n_steps,),
        in_specs=[
            pl.BlockSpec((m, d), lambda j: (0, 0)),
            pl.BlockSpec((1, d), lambda j: (0, 0)),
            pl.BlockSpec((d, KV_TN), lambda j: (0, j)),
        ] + side_specs,
        out_specs=[pl.BlockSpec((m, KV_TN), lambda j: (0, j))] + side_specs,
        scratch_shapes=[pltpu.VMEM((m, d), BF16)],
        compiler_params=pltpu.CompilerParams(
            dimension_semantics=("arbitrary",), vmem_limit_bytes=VMEM_LIMIT),
        name="kv",
    )(mem, norm_g, w_kv, *side)
    return outs[0], outs[1:]


def _attn_kernel(x_ref, g_ref, wq_ref, k_ref, v_ref, wo_ref, o_ref, oh_ref):
    h = _rms(x_ref[...], g_ref[...]).astype(BF16)
    q = _dot(h, wq_ref[...]).astype(BF16)
    for hd in range(XA_HEADS):
        cols = pl.ds(hd * XA_HEAD_DIM, XA_HEAD_DIM)
        qh = q[:, hd * XA_HEAD_DIM:(hd + 1) * XA_HEAD_DIM]
        s = lax.dot_general(qh, k_ref[0, :, cols], (((1,), (1,)), ((), ())), preferred_element_type=F32)
        s = s * (XA_HEAD_DIM ** -0.5)
        e = jnp.exp(s - jnp.max(s, axis=-1, keepdims=True))
        p = (e / jnp.sum(e, axis=-1, keepdims=True)).astype(BF16)
        oh_ref[:, cols] = _dot(p, v_ref[0, :, cols]).astype(BF16)
    o_ref[...] = x_ref[...] + _dot(oh_ref[...], wo_ref[...])


def _attn(x, norm_g, w_q, kv, w_o, *, seq):
    t, d = x.shape
    tm = ATT_TM
    bps = seq // tm
    n_mem = kv.shape[1]
    return pl.pallas_call(
        _attn_kernel,
        out_shape=jax.ShapeDtypeStruct((t, d), F32),
        grid=(t // tm,),
        in_specs=[
            pl.BlockSpec((tm, d), lambda i: (i, 0)),
            _resident(norm_g),
            _resident(w_q),
            pl.BlockSpec((1, n_mem, d), lambda i: (i // bps, 0, 0)),
            pl.BlockSpec((1, n_mem, d), lambda i: (i // bps, 0, 1)),
            _resident(w_o),
        ],
        out_specs=pl.BlockSpec((tm, d), lambda i: (i, 0)),
        scratch_shapes=[pltpu.VMEM((tm, d), BF16)],
        compiler_params=pltpu.CompilerParams(
            dimension_semantics=("arbitrary",), vmem_limit_bytes=VMEM_LIMIT),
        name="attn",
    )(x, norm_g, w_q, kv, kv, w_o)


def kernel(x, mem, ffn1_norm, ffn1_w_in, ffn1_w_out, mix_norm, w_mix_in, conv_w, conv_b, conv_ln_g, conv_ln_b, sgu_ln_g, sgu_ln_b, sgu_w, sgu_b, out_norm_conv, out_norm_sgu, w_mix_out, xattn_norm, mem_norm, w_q, w_kv, w_o, ffn2_norm, ffn2_w_in, ffn2_w_out, final_norm):
    b, s, d = x.shape
    n_mem = mem.shape[1]
    depth = ffn1_norm.shape[0]
    row = lambda v: v.reshape(1, -1)
    ones = jnp.ones((1, d), F32)

    xs = x.reshape(b * s, d)
    mem2 = mem.reshape(b * n_mem, d)
    for l in range(depth):
        last = l == depth - 1
        xs, (w_mix_in_b, w_mix_out_b) = _ffn(
            xs, row(ffn1_norm[l]), ffn1_w_in[l], ffn1_w_out[l], ones, final_norm=False,
            cast_to_bf16=(w_mix_in[l], w_mix_out[l]))
        sgu_bias = jnp.repeat(jnp.transpose(sgu_b[l]), SGU_GROUP_DIM, axis=1)
        xs = _mix(xs, row(mix_norm[l]), w_mix_in_b, w_mix_out_b,
                  conv_w[l], row(conv_b[l]), row(conv_ln_g[l]), row(conv_ln_b[l]),
                  row(sgu_ln_g[l]), row(sgu_ln_b[l]), sgu_w[l], sgu_bias,
                  row(out_norm_conv[l]), row(out_norm_sgu[l]), seq=s)
        kv, (w_q_b, w_o_b) = _kv(mem2, row(mem_norm[l]), w_kv[l], cast_to_bf16=(w_q[l], w_o[l]))
        xs = _attn(xs, row(xattn_norm[l]), w_q_b, kv.reshape(b, n_mem, 2 * d), w_o_b, seq=s)
        xs, _ = _ffn(xs, row(ffn2_norm[l]), ffn2_w_in[l], ffn2_w_out[l],
                     row(final_norm) if last else ones, final_norm=last)
    return xs.reshape(b, s, d)
```

```python
import functools

import jax
import jax.numpy as jnp
from jax import lax
from jax.experimental import pallas as pl
from jax.experimental.pallas import tpu as pltpu

D_MODEL = 2048
D_FF = 5632
CONV_CH = 1024
CONV_WIDTH = 31
SGU_CH = 1024
SGU_GROUPS = 8
SGU_GROUP_DIM = SGU_CH // SGU_GROUPS
GMLP_CHUNK = 128
CHUNK = 64
XA_HEADS = 4
XA_HEAD_DIM = D_MODEL // XA_HEADS
MACARON_SCALE = 0.5
RMS_EPS = 1e-6
LN_EPS = 1e-5

F32 = jnp.float32
BF16 = jnp.bfloat16

LANES = 128
SUBLANES = 8
HALO = 32
VMEM_LIMIT = 62 * 1024 * 1024

FFN_TM, FFN_TF = 1024, 512
MIX_TM = 512
ATT_TM = 512
KV_TN = 512
CONV_ROWS = 64
EXT_PITCH = 2
GLU_TILES = 2
ROW_CHUNK = 128
CAST_ROWS = 32


def _rms(x, g):
    ms = jnp.mean(x * x, axis=-1, keepdims=True)
    return x * lax.rsqrt(ms + RMS_EPS) * g


def _layer_norm(x, g, b):
    mu = jnp.mean(x, axis=-1, keepdims=True)
    xc = x - mu
    var = jnp.mean(xc * xc, axis=-1, keepdims=True)
    return xc * lax.rsqrt(var + LN_EPS) * g + b


def _silu(x):
    return x * jax.nn.sigmoid(x)


def _gelu(x):
    return 0.5 * x * (1.0 + lax.erf(x * (0.5 ** 0.5)))


def _dot(a, b):
    return jnp.dot(a, b, preferred_element_type=F32)


def _for_row_chunks(n_rows, chunk, body):
    def step(r, carry):
        body(pl.ds(pl.multiple_of(r * chunk, chunk), chunk))
        return carry
    lax.fori_loop(0, n_rows // chunk, step, 0)


def _resident(a):
    return pl.BlockSpec(a.shape, lambda *_: (0,) * a.ndim, pipeline_mode=pl.Buffered(1))


def _ffn_kernel(x_hbm, g_ref, wg_ref, wu_ref, wo_ref, fin_ref, *rest, final_norm, side_orders):
    n_side = len(side_orders)
    side_in, o_ref, side_out = rest[:n_side], rest[n_side], rest[n_side + 1:2 * n_side + 1]
    h_ref, xbuf, xsem = rest[2 * n_side + 1:]
    i = pl.program_id(0)
    j = pl.program_id(1)
    tm = o_ref.shape[0]

    def x_copy(block):
        rows = pl.ds(pl.multiple_of(block * tm, tm), tm)
        return pltpu.make_async_copy(x_hbm.at[rows], xbuf, xsem.at[0])

    @pl.when((i == 0) & (j == 0))
    def _():
        x_copy(0).start()

    def hidden_tile(h):
        gate = _dot(h, wg_ref[...].astype(BF16))
        up = _dot(h, wu_ref[...].astype(BF16))
        a = (MACARON_SCALE * _silu(gate) * up).astype(BF16)
        out = _dot(a, wo_ref[...].astype(BF16))
        for src, dst, order in zip(side_in, side_out, side_orders):
            if order is None:
                dst[...] = src[...].astype(BF16)
            else:
                for to_tile, from_tile in enumerate(order):
                    dst[:, pl.ds(to_tile * LANES, LANES)] = src[:, pl.ds(from_tile * LANES, LANES)].astype(BF16)
        return out

    @pl.when(j == 0)
    def _():
        x_copy(i).wait()
        x = xbuf[...]
        h = _rms(x, g_ref[...]).astype(BF16)
        h_ref[...] = h
        o_ref[...] = x + hidden_tile(h)

    @pl.when((j == 1) & (i + 1 < pl.num_programs(0)))
    def _():
        x_copy(i + 1).start()

    @pl.when(j > 0)
    def _():
        o_ref[...] += hidden_tile(h_ref[...])

    if final_norm:
        @pl.when(j == pl.num_programs(1) - 1)
        def _():
            def norm_rows(rows):
                o_ref[rows, :] = _rms(o_ref[rows, :], fin_ref[...])
            _for_row_chunks(tm, ROW_CHUNK, norm_rows)


def _ffn(x, norm_g, w_in, w_out, fin_g, *, final_norm, cast_to_bf16=()):
    t, d = x.shape
    f = w_out.shape[0]
    tm, tf = FFN_TM, FFN_TF
    nf = f // tf
    n_steps = (t // tm) * nf
    assert nf >= 2, "the next row block is prefetched at hidden tile 1"
    side = [w for w, _ in cast_to_bf16]
    side_orders = tuple(None if order is None else tuple(order) for _, order in cast_to_bf16)

    def slab_spec(w):
        n_slabs = w.shape[0] // CAST_ROWS
        assert w.shape[0] % CAST_ROWS == 0 and n_slabs <= n_steps
        return pl.BlockSpec((CAST_ROWS, w.shape[1]), lambda i, j: (jnp.minimum(i * nf + j, n_slabs - 1), 0))

    side_specs = [slab_spec(w) for w in side]
    outs = pl.pallas_call(
        functools.partial(_ffn_kernel, final_norm=final_norm, side_orders=side_orders),
        out_shape=[jax.ShapeDtypeStruct((t, d), F32)] + [jax.ShapeDtypeStruct(w.shape, BF16) for w in side],
        grid=(t // tm, nf),
        in_specs=[
            pl.BlockSpec(memory_space=pl.ANY),
            pl.BlockSpec((1, d), lambda i, j: (0, 0)),
            pl.BlockSpec((d, tf), lambda i, j: (0, j)),
            pl.BlockSpec((d, tf), lambda i, j: (0, j + nf)),
            pl.BlockSpec((tf, d), lambda i, j: (j, 0)),
            pl.BlockSpec((1, d), lambda i, j: (0, 0)),
        ] + side_specs,
        out_specs=[pl.BlockSpec((tm, d), lambda i, j: (i, 0))] + side_specs,
        scratch_shapes=[
            pltpu.VMEM((tm, d), BF16),
            pltpu.VMEM((tm, d), F32),
            pltpu.SemaphoreType.DMA((1,)),
        ],
        compiler_params=pltpu.CompilerParams(
            dimension_semantics=("arbitrary", "arbitrary"), vmem_limit_bytes=VMEM_LIMIT),
        name="ffn_final" if final_norm else "ffn",
    )(x, norm_g, w_in, w_in, w_out, fin_g, *side)
    return outs[0], outs[1:]


def _glu_paired_lane_tiles():
    n = CONV_CH // LANES
    order = []
    for c in range(n):
        order += [c, n + c]
    return order + list(range(2 * n, 2 * n + 2 * SGU_CH // LANES))


def _conv_unit(ext_ref, w_ref, y_ref, row0, c):
    n = CONV_ROWS
    first = HALO - (CONV_WIDTH - 1)
    lanes = pl.ds(c * LANES, LANES)
    acc = jnp.zeros((n, LANES), F32)
    for k in range(CONV_WIDTH):
        window = ext_ref[c, pl.ds(EXT_PITCH * (row0 + first + k), n, stride=EXT_PITCH), :]
        acc = acc + w_ref[pl.ds(k, 1), lanes] * window
    y_ref[pl.ds(row0, n), lanes] = acc


def _mix_kernel(x_ref, g_ref, win_ref, wout_ref, convw_ref, convb_ref, clng_ref, clnb_ref,
                slng_ref, slnb_ref, sw_ref, sbias_ref, onc_ref, ons_ref,
                o_ref, ext_ref, y_ref, vb_ref, *, blocks_per_seq):
    i = pl.program_id(0)
    tm = o_ref.shape[0]

    n_tiles = CONV_CH // LANES
    halo_rows = EXT_PITCH * HALO

    @pl.when(i % blocks_per_seq == 0)
    def _():
        ext_ref[:, pl.ds(0, halo_rows), :] = jnp.zeros((n_tiles, halo_rows, LANES), F32)

    h = _rms(x_ref[...], g_ref[...]).astype(BF16)

    slice_cols = 2 * LANES * GLU_TILES
    for piece in range(n_tiles // GLU_TILES):
        p = _dot(h, win_ref[:, pl.ds(piece * slice_cols, slice_cols)])
        for t in range(GLU_TILES):
            val = p[:, 2 * LANES * t:2 * LANES * t + LANES]
            gate = p[:, 2 * LANES * t + LANES:2 * LANES * (t + 1)]
            ext_ref[piece * GLU_TILES + t, pl.ds(halo_rows, tm, stride=EXT_PITCH), :] = val * jax.nn.sigmoid(gate)
        for t in range(GLU_TILES):
            for r in range(tm // CONV_ROWS):
                _conv_unit(ext_ref, convw_ref, y_ref, r * CONV_ROWS, piece * GLU_TILES + t)
    q = _dot(h, win_ref[:, pl.ds(2 * CONV_CH, 2 * SGU_CH)])
    for c in range(n_tiles):
        tail = ext_ref[c, pl.ds(EXT_PITCH * tm, HALO, stride=EXT_PITCH), :]
        ext_ref[c, pl.ds(0, HALO, stride=EXT_PITCH), :] = tail

    y = y_ref[...] + convb_ref[...]
    y = _silu(_layer_norm(y, clng_ref[...], clnb_ref[...]))
    vb_ref[:, pl.ds(0, CONV_CH)] = _rms(y, onc_ref[...]).astype(BF16)

    u = _gelu(q[:, :SGU_CH])
    v = _layer_norm(_gelu(q[:, SGU_CH:]), slng_ref[...], slnb_ref[...]).astype(BF16)
    blk = lax.broadcasted_iota(jnp.int32, (GMLP_CHUNK, GMLP_CHUNK), 0) // CHUNK
    blk_t = lax.broadcasted_iota(jnp.int32, (GMLP_CHUNK, GMLP_CHUNK), 1) // CHUNK
    mask = blk_t <= blk
    ws = [jnp.where(mask, sw_ref[hd], 0.0).astype(BF16) for hd in range(SGU_GROUPS)]
    for n in range(tm // GMLP_CHUNK):
        r0 = n * GMLP_CHUNK
        tiles = []
        for hd in range(SGU_GROUPS):
            c0 = hd * SGU_GROUP_DIM
            mixed = _dot(ws[hd], v[r0:r0 + GMLP_CHUNK, c0:c0 + SGU_GROUP_DIM])
            mixed = mixed + sbias_ref[:, pl.ds(c0, SGU_GROUP_DIM)]
            tiles.append(u[r0:r0 + GMLP_CHUNK, c0:c0 + SGU_GROUP_DIM] * mixed)
        gated = jnp.concatenate(tiles, axis=1)
        vb_ref[pl.ds(r0, GMLP_CHUNK), pl.ds(CONV_CH, SGU_CH)] = _rms(gated, ons_ref[...]).astype(BF16)

    o_ref[...] = x_ref[...] + _dot(vb_ref[...], wout_ref[...])


def _mix(x, norm_g, w_in, w_out, conv_w, conv_b, cln_g, cln_b, sln_g, sln_b, sgu_w, sgu_bias,
         on_conv, on_sgu, *, seq):
    t, d = x.shape
    tm = MIX_TM
    args = (norm_g, w_in, w_out, conv_w, conv_b, cln_g, cln_b, sln_g, sln_b, sgu_w, sgu_bias,
            on_conv, on_sgu)
    return pl.pallas_call(
        functools.partial(_mix_kernel, blocks_per_seq=seq // tm),
        out_shape=jax.ShapeDtypeStruct((t, d), F32),
        grid=(t // tm,),
        in_specs=[pl.BlockSpec((tm, d), lambda i: (i, 0))] + [_resident(a) for a in args],
        out_specs=pl.BlockSpec((tm, d), lambda i: (i, 0)),
        scratch_shapes=[
            pltpu.VMEM((CONV_CH // LANES, EXT_PITCH * (HALO + tm), LANES), F32),
            pltpu.VMEM((tm, CONV_CH), F32),
            pltpu.VMEM((tm, CONV_CH + SGU_CH), BF16),
        ],
        compiler_params=pltpu.CompilerParams(
            dimension_semantics=("arbitrary",), vmem_limit_bytes=VMEM_LIMIT),
        name="mix",
    )(x, *args)


def _kv_kernel(m_ref, g_ref, w_ref, *rest, n_side):
    side_in, o_ref, side_out, h_ref = rest[:n_side], rest[n_side], rest[n_side + 1:2 * n_side + 1], rest[-1]

    @pl.when(pl.program_id(0) == 0)
    def _():
        h_ref[...] = _rms(m_ref[...], g_ref[...]).astype(BF16)
    o_ref[...] = _dot(h_ref[...], w_ref[...].astype(BF16)).astype(BF16)
    for src, dst in zip(side_in, side_out):
        dst[...] = src[...].astype(BF16)


def _kv(mem, norm_g, w_kv, cast_to_bf16=()):
    m, d = mem.shape
    n = w_kv.shape[1]
    n_steps = n // KV_TN
    side = list(cast_to_bf16)

    def slab_spec(w):
        assert w.shape[0] % n_steps == 0
        return pl.BlockSpec((w.shape[0] // n_steps, w.shape[1]), lambda j: (j, 0))

    side_specs = [slab_spec(w) for w in side]
    outs = pl.pallas_call(
        functools.partial(_kv_kernel, n_side=len(side)),
        out_shape=[jax.ShapeDtypeStruct((m, n), BF16)] + [jax.ShapeDtypeStruct(w.shape, BF16) for w in side],
        grid=(n_steps,),
        in_specs=[
            pl.BlockSpec((m, d), lambda j: (0, 0)),
            pl.BlockSpec((1, d), lambda j: (0, 0)),
            pl.BlockSpec((d, KV_TN), lambda j: (0, j)),
        ] + side_specs,
        out_specs=[pl.BlockSpec((m, KV_TN), lambda j: (0, j))] + side_specs,
        scratch_shapes=[pltpu.VMEM((m, d), BF16)],
        compiler_params=pltpu.CompilerParams(
            dimension_semantics=("arbitrary",), vmem_limit_bytes=VMEM_LIMIT),
        name="kv",
    )(mem, norm_g, w_kv, *side)
    return outs[0], outs[1:]


def _attn_kernel(x_ref, g_ref, wq_ref, k_ref, v_ref, wo_ref, o_ref, oh_ref):
    h = _rms(x_ref[...], g_ref[...]).astype(BF16)
    q = _dot(h, wq_ref[...]).astype(BF16)
    for hd in range(XA_HEADS):
        cols = pl.ds(hd * XA_HEAD_DIM, XA_HEAD_DIM)
        qh = q[:, hd * XA_HEAD_DIM:(hd + 1) * XA_HEAD_DIM]
        s = lax.dot_general(qh, k_ref[0, :, cols], (((1,), (1,)), ((), ())), preferred_element_type=F32)
        s = s * (XA_HEAD_DIM ** -0.5)
        e = jnp.exp(s - jnp.max(s, axis=-1, keepdims=True))
        p = (e / jnp.sum(e, axis=-1, keepdims=True)).astype(BF16)
        oh_ref[:, cols] = _dot(p, v_ref[0, :, cols]).astype(BF16)
    o_ref[...] = x_ref[...] + _dot(oh_ref[...], wo_ref[...])


def _attn(x, norm_g, w_q, kv, w_o, *, seq):
    t, d = x.shape
    tm = ATT_TM
    bps = seq // tm
    n_mem = kv.shape[1]
    return pl.pallas_call(
        _attn_kernel,
        out_shape=jax.ShapeDtypeStruct((t, d), F32),
        grid=(t // tm,),
        in_specs=[
            pl.BlockSpec((tm, d), lambda i: (i, 0)),
            _resident(norm_g),
            _resident(w_q),
            pl.BlockSpec((1, n_mem, d), lambda i: (i // bps, 0, 0)),
            pl.BlockSpec((1, n_mem, d), lambda i: (i // bps, 0, 1)),
            _resident(w_o),
        ],
        out_specs=pl.BlockSpec((tm, d), lambda i: (i, 0)),
        scratch_shapes=[pltpu.VMEM((tm, d), BF16)],
        compiler_params=pltpu.CompilerParams(
            dimension_semantics=("arbitrary",), vmem_limit_bytes=VMEM_LIMIT),
        name="attn",
    )(x, norm_g, w_q, kv, kv, w_o)


def kernel(x, mem, ffn1_norm, ffn1_w_in, ffn1_w_out, mix_norm, w_mix_in, conv_w, conv_b, conv_ln_g, conv_ln_b, sgu_ln_g, sgu_ln_b, sgu_w, sgu_b, out_norm_conv, out_norm_sgu, w_mix_out, xattn_norm, mem_norm, w_q, w_kv, w_o, ffn2_norm, ffn2_w_in, ffn2_w_out, final_norm):
    b, s, d = x.shape
    n_mem = mem.shape[1]
    depth = ffn1_norm.shape[0]
    row = lambda v: v.reshape(1, -1)
    ones = jnp.ones((1, d), F32)

    xs = x.reshape(b * s, d)
    mem2 = mem.reshape(b * n_mem, d)
    for l in range(depth):
        last = l == depth - 1
        xs, (w_mix_in_b, w_mix_out_b) = _ffn(
            xs, row(ffn1_norm[l]), ffn1_w_in[l], ffn1_w_out[l], ones, final_norm=False,
            cast_to_bf16=((w_mix_in[l], _glu_paired_lane_tiles()), (w_mix_out[l], None)))
        sgu_bias = jnp.repeat(jnp.transpose(sgu_b[l]), SGU_GROUP_DIM, axis=1)
        xs = _mix(xs, row(mix_norm[l]), w_mix_in_b, w_mix_out_b,
                  conv_w[l], row(conv_b[l]), row(conv_ln_g[l]), row(conv_ln_b[l]),
                  row(sgu_ln_g[l]), row(sgu_ln_b[l]), sgu_w[l], sgu_bias,
                  row(out_norm_conv[l]), row(out_norm_sgu[l]), seq=s)
        kv, (w_q_b, w_o_b) = _kv(mem2, row(mem_norm[l]), w_kv[l], cast_to_bf16=(w_q[l], w_o[l]))
        xs = _attn(xs, row(xattn_norm[l]), w_q_b, kv.reshape(b, n_mem, 2 * d), w_o_b, seq=s)
        xs, _ = _ffn(xs, row(ffn2_norm[l]), ffn2_w_in[l], ffn2_w_out[l],
                     row(final_norm) if last else ones, final_norm=last)
    return xs.reshape(b, s, d)
```

```python
import functools

import jax
import jax.numpy as jnp
from jax import lax
from jax.experimental import pallas as pl
from jax.experimental.pallas import tpu as pltpu

D_MODEL = 2048
D_FF = 5632
CONV_CH = 1024
CONV_WIDTH = 31
SGU_CH = 1024
SGU_GROUPS = 8
SGU_GROUP_DIM = SGU_CH // SGU_GROUPS
GMLP_CHUNK = 128
CHUNK = 64
XA_HEADS = 4
XA_HEAD_DIM = D_MODEL // XA_HEADS
MACARON_SCALE = 0.5
RMS_EPS = 1e-6
LN_EPS = 1e-5

F32 = jnp.float32
BF16 = jnp.bfloat16

LANES = 128
SUBLANES = 8
HALO = 32
VMEM_LIMIT = 62 * 1024 * 1024

FFN_TM, FFN_TF = 1024, 512
MIX_TM = 512
ATT_TM = 512
KV_TN = 512
CONV_ROWS = 64
EXT_PITCH = 2
GLU_TILES = 2
ROW_CHUNK = 128
CAST_ROWS = 32


def _rms(x, g):
    ms = jnp.mean(x * x, axis=-1, keepdims=True)
    return x * lax.rsqrt(ms + RMS_EPS) * g


def _layer_norm(x, g, b):
    mu = jnp.mean(x, axis=-1, keepdims=True)
    xc = x - mu
    var = jnp.mean(xc * xc, axis=-1, keepdims=True)
    return xc * lax.rsqrt(var + LN_EPS) * g + b


def _silu(x):
    return x * jax.nn.sigmoid(x)


def _gelu(x):
    return 0.5 * x * (1.0 + lax.erf(x * (0.5 ** 0.5)))


def _dot(a, b):
    return jnp.dot(a, b, preferred_element_type=F32)


def _for_row_chunks(n_rows, chunk, body):
    def step(r, carry):
        body(pl.ds(pl.multiple_of(r * chunk, chunk), chunk))
        return carry
    lax.fori_loop(0, n_rows // chunk, step, 0)


def _resident(a):
    return pl.BlockSpec(a.shape, lambda *_: (0,) * a.ndim, pipeline_mode=pl.Buffered(1))


def _ffn_kernel(x_hbm, g_ref, wg_ref, wu_ref, wo_ref, fin_ref, *rest, final_norm, side_orders):
    n_side = len(side_orders)
    side_in, o_ref, side_out = rest[:n_side], rest[n_side], rest[n_side + 1:2 * n_side + 1]
    h_ref, xbuf, xsem = rest[2 * n_side + 1:]
    i = pl.program_id(0)
    j = pl.program_id(1)
    tm = o_ref.shape[0]

    def x_copy(block):
        rows = pl.ds(pl.multiple_of(block * tm, tm), tm)
        return pltpu.make_async_copy(x_hbm.at[rows], xbuf, xsem.at[0])

    @pl.when((i == 0) & (j == 0))
    def _():
        x_copy(0).start()

    def hidden_tile(h):
        gate = _dot(h, wg_ref[...].astype(BF16))
        up = _dot(h, wu_ref[...].astype(BF16))
        a = (MACARON_SCALE * _silu(gate) * up).astype(BF16)
        out = _dot(a, wo_ref[...].astype(BF16))
        for src, dst, order in zip(side_in, side_out, side_orders):
            if order is None:
                dst[...] = src[...].astype(BF16)
            else:
                for to_tile, from_tile in enumerate(order):
                    dst[:, pl.ds(to_tile * LANES, LANES)] = src[:, pl.ds(from_tile * LANES, LANES)].astype(BF16)
        return out

    @pl.when(j == 0)
    def _():
        x_copy(i).wait()
        x = xbuf[...]
        h = _rms(x, g_ref[...]).astype(BF16)
        h_ref[...] = h
        o_ref[...] = x + hidden_tile(h)

    @pl.when((j == 1) & (i + 1 < pl.num_programs(0)))
    def _():
        x_copy(i + 1).start()

    @pl.when(j > 0)
    def _():
        o_ref[...] += hidden_tile(h_ref[...])

    if final_norm:
        @pl.when(j == pl.num_programs(1) - 1)
        def _():
            def norm_rows(rows):
                o_ref[rows, :] = _rms(o_ref[rows, :], fin_ref[...])
            _for_row_chunks(tm, ROW_CHUNK, norm_rows)


def _ffn(x, norm_g, w_in, w_out, fin_g, *, final_norm, cast_to_bf16=()):
    t, d = x.shape
    f = w_out.shape[0]
    tm, tf = FFN_TM, FFN_TF
    nf = f // tf
    n_steps = (t // tm) * nf
    assert nf >= 2, "the next row block is prefetched at hidden tile 1"
    side = [w for w, _ in cast_to_bf16]
    side_orders = tuple(None if order is None else tuple(order) for _, order in cast_to_bf16)

    def slab_spec(w):
        n_slabs = w.shape[0] // CAST_ROWS
        assert w.shape[0] % CAST_ROWS == 0 and n_slabs <= n_steps
        return pl.BlockSpec((CAST_ROWS, w.shape[1]), lambda i, j: (jnp.minimum(i * nf + j, n_slabs - 1), 0))

    side_specs = [slab_spec(w) for w in side]
    outs = pl.pallas_call(
        functools.partial(_ffn_kernel, final_norm=final_norm, side_orders=side_orders),
        out_shape=[jax.ShapeDtypeStruct((t, d), F32)] + [jax.ShapeDtypeStruct(w.shape, BF16) for w in side],
        grid=(t // tm, nf),
        in_specs=[
            pl.BlockSpec(memory_space=pl.ANY),
            pl.BlockSpec((1, d), lambda i, j: (0, 0)),
            pl.BlockSpec((d, tf), lambda i, j: (0, j)),
            pl.BlockSpec((d, tf), lambda i, j: (0, j + nf)),
            pl.BlockSpec((tf, d), lambda i, j: (j, 0)),
            pl.BlockSpec((1, d), lambda i, j: (0, 0)),
        ] + side_specs,
        out_specs=[pl.BlockSpec((tm, d), lambda i, j: (i, 0))] + side_specs,
        scratch_shapes=[
            pltpu.VMEM((tm, d), BF16),
            pltpu.VMEM((tm, d), F32),
            pltpu.SemaphoreType.DMA((1,)),
        ],
        compiler_params=pltpu.CompilerParams(
            dimension_semantics=("arbitrary", "arbitrary"), vmem_limit_bytes=VMEM_LIMIT),
        name="ffn_final" if final_norm else "ffn",
    )(x, norm_g, w_in, w_in, w_out, fin_g, *side)
    return outs[0], outs[1:]


def _glu_paired_lane_tiles():
    n = CONV_CH // LANES
    order = []
    for c in range(n):
        order += [c, n + c]
    return order + list(range(2 * n, 2 * n + 2 * SGU_CH // LANES))


def _conv_unit(ext_ref, w_ref, y_ref, row0, c):
    n = CONV_ROWS
    first = HALO - (CONV_WIDTH - 1)
    lanes = pl.ds(c * LANES, LANES)
    acc = jnp.zeros((n, LANES), F32)
    for k in range(CONV_WIDTH):
        window = ext_ref[c, pl.ds(EXT_PITCH * (row0 + first + k), n, stride=EXT_PITCH), :]
        acc = acc + w_ref[pl.ds(k, 1), lanes] * window
    y_ref[pl.ds(row0, n), lanes] = acc


def _mix_kernel(x_ref, g_ref, win_ref, wout_ref, convw_ref, convb_ref, clng_ref, clnb_ref,
                slng_ref, slnb_ref, sw_ref, sbias_ref, onc_ref, ons_ref, *rest, blocks_per_seq, n_side):
    side_in, o_ref, side_out = rest[:n_side], rest[n_side], rest[n_side + 1:2 * n_side + 1]
    ext_ref, y_ref, vb_ref = rest[2 * n_side + 1:]
    i = pl.program_id(0)
    tm = o_ref.shape[0]

    for src, dst in zip(side_in, side_out):
        dst[...] = src[...].astype(BF16)

    n_tiles = CONV_CH // LANES
    halo_rows = EXT_PITCH * HALO

    @pl.when(i % blocks_per_seq == 0)
    def _():
        ext_ref[:, pl.ds(0, halo_rows), :] = jnp.zeros((n_tiles, halo_rows, LANES), F32)

    h = _rms(x_ref[...], g_ref[...]).astype(BF16)

    slice_cols = 2 * LANES * GLU_TILES
    for piece in range(n_tiles // GLU_TILES):
        p = _dot(h, win_ref[:, pl.ds(piece * slice_cols, slice_cols)])
        for t in range(GLU_TILES):
            val = p[:, 2 * LANES * t:2 * LANES * t + LANES]
            gate = p[:, 2 * LANES * t + LANES:2 * LANES * (t + 1)]
            ext_ref[piece * GLU_TILES + t, pl.ds(halo_rows, tm, stride=EXT_PITCH), :] = val * jax.nn.sigmoid(gate)
        for t in range(GLU_TILES):
            for r in range(tm // CONV_ROWS):
                _conv_unit(ext_ref, convw_ref, y_ref, r * CONV_ROWS, piece * GLU_TILES + t)
    q = _dot(h, win_ref[:, pl.ds(2 * CONV_CH, 2 * SGU_CH)])
    for c in range(n_tiles):
        tail = ext_ref[c, pl.ds(EXT_PITCH * tm, HALO, stride=EXT_PITCH), :]
        ext_ref[c, pl.ds(0, HALO, stride=EXT_PITCH), :] = tail

    y = y_ref[...] + convb_ref[...]
    y = _silu(_layer_norm(y, clng_ref[...], clnb_ref[...]))
    vb_ref[:, pl.ds(0, CONV_CH)] = _rms(y, onc_ref[...]).astype(BF16)

    u = _gelu(q[:, :SGU_CH])
    v = _layer_norm(_gelu(q[:, SGU_CH:]), slng_ref[...], slnb_ref[...]).astype(BF16)
    blk = lax.broadcasted_iota(jnp.int32, (GMLP_CHUNK, GMLP_CHUNK), 0) // CHUNK
    blk_t = lax.broadcasted_iota(jnp.int32, (GMLP_CHUNK, GMLP_CHUNK), 1) // CHUNK
    mask = blk_t <= blk
    ws = [jnp.where(mask, sw_ref[hd], 0.0).astype(BF16) for hd in range(SGU_GROUPS)]
    for n in range(tm // GMLP_CHUNK):
        r0 = n * GMLP_CHUNK
        tiles = []
        for hd in range(SGU_GROUPS):
            c0 = hd * SGU_GROUP_DIM
            mixed = _dot(ws[hd], v[r0:r0 + GMLP_CHUNK, c0:c0 + SGU_GROUP_DIM])
            mixed = mixed + sbias_ref[:, pl.ds(c0, SGU_GROUP_DIM)]
            tiles.append(u[r0:r0 + GMLP_CHUNK, c0:c0 + SGU_GROUP_DIM] * mixed)
        gated = jnp.concatenate(tiles, axis=1)
        vb_ref[pl.ds(r0, GMLP_CHUNK), pl.ds(CONV_CH, SGU_CH)] = _rms(gated, ons_ref[...]).astype(BF16)

    o_ref[...] = x_ref[...] + _dot(vb_ref[...], wout_ref[...])


def _mix(x, norm_g, w_in, w_out, conv_w, conv_b, cln_g, cln_b, sln_g, sln_b, sgu_w, sgu_bias,
         on_conv, on_sgu, *, seq, cast_to_bf16=()):
    t, d = x.shape
    tm = MIX_TM
    n_steps = t // tm
    args = (norm_g, w_in, w_out, conv_w, conv_b, cln_g, cln_b, sln_g, sln_b, sgu_w, sgu_bias,
            on_conv, on_sgu)
    side = list(cast_to_bf16)

    def slab_spec(w):
        assert w.shape[0] % n_steps == 0
        return pl.BlockSpec((w.shape[0] // n_steps, w.shape[1]), lambda i: (i, 0))

    side_specs = [slab_spec(w) for w in side]
    outs = pl.pallas_call(
        functools.partial(_mix_kernel, blocks_per_seq=seq // tm, n_side=len(side)),
        out_shape=[jax.ShapeDtypeStruct((t, d), F32)] + [jax.ShapeDtypeStruct(w.shape, BF16) for w in side],
        grid=(n_steps,),
        in_specs=[pl.BlockSpec((tm, d), lambda i: (i, 0))] + [_resident(a) for a in args] + side_specs,
        out_specs=[pl.BlockSpec((tm, d), lambda i: (i, 0))] + side_specs,
        scratch_shapes=[
            pltpu.VMEM((CONV_CH // LANES, EXT_PITCH * (HALO + tm), LANES), F32),
            pltpu.VMEM((tm, CONV_CH), F32),
            pltpu.VMEM((tm, CONV_CH + SGU_CH), BF16),
        ],
        compiler_params=pltpu.CompilerParams(
            dimension_semantics=("arbitrary",), vmem_limit_bytes=VMEM_LIMIT),
        name="mix",
    )(x, *args, *side)
    return outs[0], outs[1:]


def _kv_kernel(m_ref, g_ref, w_ref, *rest, n_side):
    side_in, o_ref, side_out, h_ref = rest[:n_side], rest[n_side], rest[n_side + 1:2 * n_side + 1], rest[-1]

    @pl.when(pl.program_id(0) == 0)
    def _():
        h_ref[...] = _rms(m_ref[...], g_ref[...]).astype(BF16)
    o_ref[...] = _dot(h_ref[...], w_ref[...].astype(BF16)).astype(BF16)
    for src, dst in zip(side_in, side_out):
        dst[...] = src[...].astype(BF16)


def _kv(mem, norm_g, w_kv, cast_to_bf16=()):
    m, d = mem.shape
    n = w_kv.shape[1]
    n_steps = n // KV_TN
    side = list(cast_to_bf16)

    def slab_spec(w):
        assert w.shape[0] % n_steps == 0
        return pl.BlockSpec((w.shape[0] // n_steps, w.shape[1]), lambda j: (j, 0))

    side_specs = [slab_spec(w) for w in side]
    outs = pl.pallas_call(
        functools.partial(_kv_kernel, n_side=len(side)),
        out_shape=[jax.ShapeDtypeStruct((m, n), BF16)] + [jax.ShapeDtypeStruct(w.shape, BF16) for w in side],
        grid=(n_steps,),
        in_specs=[
            pl.BlockSpec((m, d), lambda j: (0, 0)),
            pl.BlockSpec((1, d), lambda j: (0, 0)),
            pl.BlockSpec((d, KV_TN), lambda j: (0, j)),
        ] + side_specs,
        out_specs=[pl.BlockSpec((m, KV_TN), lambda j: (0, j))] + side_specs,
        scratch_shapes=[pltpu.VMEM((m, d), BF16)],
        compiler_params=pltpu.CompilerParams(
            dimension_semantics=("arbitrary",), vmem_limit_bytes=VMEM_LIMIT),
        name="kv",
    )(mem, norm_g, w_kv, *side)
    return outs[0], outs[1:]


def _attn_kernel(x_ref, g_ref, wq_ref, k_ref, v_ref, wo_ref, o_ref, oh_ref):
    h = _rms(x_ref[...], g_ref[...]).astype(BF16)
    q = _dot(h, wq_ref[...]).astype(BF16)
    for hd in range(XA_HEADS):
        cols = pl.ds(hd * XA_HEAD_DIM, XA_HEAD_DIM)
        qh = q[:, hd * XA_HEAD_DIM:(hd + 1) * XA_HEAD_DIM]
        s = lax.dot_general(qh, k_ref[0, :, cols], (((1,), (1,)), ((), ())), preferred_element_type=F32)
        s = s * (XA_HEAD_DIM ** -0.5)
        e = jnp.exp(s - jnp.max(s, axis=-1, keepdims=True))
        p = (e / jnp.sum(e, axis=-1, keepdims=True)).astype(BF16)
        oh_ref[:, cols] = _dot(p, v_ref[0, :, cols]).astype(BF16)
    o_ref[...] = x_ref[...] + _dot(oh_ref[...], wo_ref[...])


def _attn(x, norm_g, w_q, kv, w_o, *, seq):
    t, d = x.shape
    tm = ATT_TM
    bps = seq // tm
    n_mem = kv.shape[1]
    return pl.pallas_call(
        _attn_kernel,
        out_shape=jax.ShapeDtypeStruct((t, d), F32),
        grid=(t // tm,),
        in_specs=[
            pl.BlockSpec((tm, d), lambda i: (i, 0)),
            _resident(norm_g),
            _resident(w_q),
            pl.BlockSpec((1, n_mem, d), lambda i: (i // bps, 0, 0)),
            pl.BlockSpec((1, n_mem, d), lambda i: (i // bps, 0, 1)),
            _resident(w_o),
        ],
        out_specs=pl.BlockSpec((tm, d), lambda i: (i, 0)),
        scratch_shapes=[pltpu.VMEM((tm, d), BF16)],
        compiler_params=pltpu.CompilerParams(
            dimension_semantics=("arbitrary",), vmem_limit_bytes=VMEM_LIMIT),
        name="attn",
    )(x, norm_g, w_q, kv, kv, w_o)


def kernel(x, mem, ffn1_norm, ffn1_w_in, ffn1_w_out, mix_norm, w_mix_in, conv_w, conv_b, conv_ln_g, conv_ln_b, sgu_ln_g, sgu_ln_b, sgu_w, sgu_b, out_norm_conv, out_norm_sgu, w_mix_out, xattn_norm, mem_norm, w_q, w_kv, w_o, ffn2_norm, ffn2_w_in, ffn2_w_out, final_norm):
    b, s, d = x.shape
    n_mem = mem.shape[1]
    depth = ffn1_norm.shape[0]
    row = lambda v: v.reshape(1, -1)
    ones = jnp.ones((1, d), F32)

    xs = x.reshape(b * s, d)
    mem2 = mem.reshape(b * n_mem, d)
    for l in range(depth):
        last = l == depth - 1
        xs, (w_mix_in_b, w_mix_out_b) = _ffn(
            xs, row(ffn1_norm[l]), ffn1_w_in[l], ffn1_w_out[l], ones, final_norm=False,
            cast_to_bf16=((w_mix_in[l], _glu_paired_lane_tiles()), (w_mix_out[l], None)))
        sgu_bias = jnp.repeat(jnp.transpose(sgu_b[l]), SGU_GROUP_DIM, axis=1)
        xs, (w_q_b, w_o_b) = _mix(
            xs, row(mix_norm[l]), w_mix_in_b, w_mix_out_b,
            conv_w[l], row(conv_b[l]), row(conv_ln_g[l]), row(conv_ln_b[l]),
            row(sgu_ln_g[l]), row(sgu_ln_b[l]), sgu_w[l], sgu_bias,
            row(out_norm_conv[l]), row(out_norm_sgu[l]), seq=s, cast_to_bf16=(w_q[l], w_o[l]))
        kv, _ = _kv(mem2, row(mem_norm[l]), w_kv[l])
        xs = _attn(xs, row(xattn_norm[l]), w_q_b, kv.reshape(b, n_mem, 2 * d), w_o_b, seq=s)
        xs, _ = _ffn(xs, row(ffn2_norm[l]), ffn2_w_in[l], ffn2_w_out[l],
                     row(final_norm) if last else ones, final_norm=last)
    return xs.reshape(b, s, d)
```

```python
import functools

import jax
import jax.numpy as jnp
from jax import lax
from jax.experimental import pallas as pl
from jax.experimental.pallas import tpu as pltpu

D_MODEL = 2048
D_FF = 5632
CONV_CH = 1024
CONV_WIDTH = 31
SGU_CH = 1024
SGU_GROUPS = 8
SGU_GROUP_DIM = SGU_CH // SGU_GROUPS
GMLP_CHUNK = 128
CHUNK = 64
XA_HEADS = 4
XA_HEAD_DIM = D_MODEL // XA_HEADS
MACARON_SCALE = 0.5
RMS_EPS = 1e-6
LN_EPS = 1e-5

F32 = jnp.float32
BF16 = jnp.bfloat16

LANES = 128
SUBLANES = 8
HALO = 32
VMEM_LIMIT = 62 * 1024 * 1024

FFN_TM, FFN_TF = 1024, 512
MIX_TM = 512
ATT_TM = 512
KV_TN = 1024
CONV_ROWS = 64
EXT_PITCH = 2
GLU_TILES = 2
ROW_CHUNK = 128
CAST_ROWS = 32


def _rms(x, g):
    ms = jnp.mean(x * x, axis=-1, keepdims=True)
    return x * lax.rsqrt(ms + RMS_EPS) * g


def _layer_norm(x, g, b):
    mu = jnp.mean(x, axis=-1, keepdims=True)
    xc = x - mu
    var = jnp.mean(xc * xc, axis=-1, keepdims=True)
    return xc * lax.rsqrt(var + LN_EPS) * g + b


def _silu(x):
    return x * jax.nn.sigmoid(x)


def _gelu(x):
    return 0.5 * x * (1.0 + lax.erf(x * (0.5 ** 0.5)))


def _dot(a, b):
    return jnp.dot(a, b, preferred_element_type=F32)


def _for_row_chunks(n_rows, chunk, body):
    def step(r, carry):
        body(pl.ds(pl.multiple_of(r * chunk, chunk), chunk))
        return carry
    lax.fori_loop(0, n_rows // chunk, step, 0)


def _resident(a):
    return pl.BlockSpec(a.shape, lambda *_: (0,) * a.ndim, pipeline_mode=pl.Buffered(1))


def _cast_slabs(srcs, dsts, orders=None):
    for k, (src, dst) in enumerate(zip(srcs, dsts)):
        order = None if orders is None else orders[k]
        if order is None:
            dst[...] = src[...].astype(BF16)
        else:
            for to_tile, from_tile in enumerate(order):
                dst[:, pl.ds(to_tile * LANES, LANES)] = src[:, pl.ds(from_tile * LANES, LANES)].astype(BF16)


def _ffn_kernel(x_hbm, g_ref, wg_ref, wu_ref, wo_ref, fin_ref, *rest, final_norm, side_orders):
    n_side = len(side_orders)
    side_in, o_ref, side_out = rest[:n_side], rest[n_side], rest[n_side + 1:2 * n_side + 1]
    h_ref, xbuf, xsem = rest[2 * n_side + 1:]
    i = pl.program_id(0)
    j = pl.program_id(1)
    tm = o_ref.shape[0]

    def x_copy(block):
        rows = pl.ds(pl.multiple_of(block * tm, tm), tm)
        return pltpu.make_async_copy(x_hbm.at[rows], xbuf, xsem.at[0])

    @pl.when((i == 0) & (j == 0))
    def _():
        x_copy(0).start()

    def hidden_tile(h):
        gate = _dot(h, wg_ref[...].astype(BF16))
        up = _dot(h, wu_ref[...].astype(BF16))
        a = (MACARON_SCALE * _silu(gate) * up).astype(BF16)
        out = _dot(a, wo_ref[...].astype(BF16))
        _cast_slabs(side_in, side_out, side_orders)
        return out

    @pl.when(j == 0)
    def _():
        x_copy(i).wait()
        x = xbuf[...]
        h = _rms(x, g_ref[...]).astype(BF16)
        h_ref[...] = h
        o_ref[...] = x + hidden_tile(h)

    @pl.when((j == 1) & (i + 1 < pl.num_programs(0)))
    def _():
        x_copy(i + 1).start()

    @pl.when(j > 0)
    def _():
        o_ref[...] += hidden_tile(h_ref[...])

    if final_norm:
        @pl.when(j == pl.num_programs(1) - 1)
        def _():
            def norm_rows(rows):
                o_ref[rows, :] = _rms(o_ref[rows, :], fin_ref[...])
            _for_row_chunks(tm, ROW_CHUNK, norm_rows)


def _ffn(x, norm_g, w_in, w_out, fin_g, *, final_norm, cast_to_bf16=()):
    t, d = x.shape
    f = w_out.shape[0]
    tm, tf = FFN_TM, FFN_TF
    nf = f // tf
    n_steps = (t // tm) * nf
    assert nf >= 2, "the next row block is prefetched at hidden tile 1"
    side = [w for w, _ in cast_to_bf16]
    side_orders = tuple(None if order is None else tuple(order) for _, order in cast_to_bf16)

    def slab_spec(w):
        n_slabs = w.shape[0] // CAST_ROWS
        assert w.shape[0] % CAST_ROWS == 0 and n_slabs <= n_steps
        return pl.BlockSpec((CAST_ROWS, w.shape[1]), lambda i, j: (jnp.minimum(i * nf + j, n_slabs - 1), 0))

    side_specs = [slab_spec(w) for w in side]
    outs = pl.pallas_call(
        functools.partial(_ffn_kernel, final_norm=final_norm, side_orders=side_orders),
        out_shape=[jax.ShapeDtypeStruct((t, d), F32)] + [jax.ShapeDtypeStruct(w.shape, BF16) for w in side],
        grid=(t // tm, nf),
        in_specs=[
            pl.BlockSpec(memory_space=pl.ANY),
            pl.BlockSpec((1, d), lambda i, j: (0, 0)),
            pl.BlockSpec((d, tf), lambda i, j: (0, j)),
            pl.BlockSpec((d, tf), lambda i, j: (0, j + nf)),
            pl.BlockSpec((tf, d), lambda i, j: (j, 0)),
            pl.BlockSpec((1, d), lambda i, j: (0, 0)),
        ] + side_specs,
        out_specs=[pl.BlockSpec((tm, d), lambda i, j: (i, 0))] + side_specs,
        scratch_shapes=[
            pltpu.VMEM((tm, d), BF16),
            pltpu.VMEM((tm, d), F32),
            pltpu.SemaphoreType.DMA((1,)),
        ],
        compiler_params=pltpu.CompilerParams(
            dimension_semantics=("arbitrary", "arbitrary"), vmem_limit_bytes=VMEM_LIMIT),
        name="ffn_final" if final_norm else "ffn",
    )(x, norm_g, w_in, w_in, w_out, fin_g, *side)
    return outs[0], outs[1:]


def _glu_paired_lane_tiles():
    n = CONV_CH // LANES
    order = []
    for c in range(n):
        order += [c, n + c]
    return order + list(range(2 * n, 2 * n + 2 * SGU_CH // LANES))


def _conv_unit(ext_ref, w_ref, y_ref, row0, c):
    n = CONV_ROWS
    first = HALO - (CONV_WIDTH - 1)
    lanes = pl.ds(c * LANES, LANES)
    acc = jnp.zeros((n, LANES), F32)
    for k in range(CONV_WIDTH):
        window = ext_ref[c, pl.ds(EXT_PITCH * (row0 + first + k), n, stride=EXT_PITCH), :]
        acc = acc + w_ref[pl.ds(k, 1), lanes] * window
    y_ref[pl.ds(row0, n), lanes] = acc


def _mix_kernel(x_ref, g_ref, win_ref, wout_ref, convw_ref, convb_ref, clng_ref, clnb_ref,
                slng_ref, slnb_ref, sw_ref, sbias_ref, onc_ref, ons_ref, *rest, blocks_per_seq, n_side):
    side_in, o_ref, side_out = rest[:n_side], rest[n_side], rest[n_side + 1:2 * n_side + 1]
    ext_ref, y_ref, vb_ref = rest[2 * n_side + 1:]
    i = pl.program_id(0)
    tm = o_ref.shape[0]
    n_tiles = CONV_CH // LANES
    halo_rows = EXT_PITCH * HALO

    @pl.when(i % blocks_per_seq == 0)
    def _():
        ext_ref[:, pl.ds(0, halo_rows), :] = jnp.zeros((n_tiles, halo_rows, LANES), F32)

    _cast_slabs(side_in, side_out)
    h = _rms(x_ref[...], g_ref[...]).astype(BF16)

    slice_cols = 2 * LANES * GLU_TILES
    for piece in range(n_tiles // GLU_TILES):
        p = _dot(h, win_ref[:, pl.ds(piece * slice_cols, slice_cols)])
        for t in range(GLU_TILES):
            val = p[:, 2 * LANES * t:2 * LANES * t + LANES]
            gate = p[:, 2 * LANES * t + LANES:2 * LANES * (t + 1)]
            ext_ref[piece * GLU_TILES + t, pl.ds(halo_rows, tm, stride=EXT_PITCH), :] = val * jax.nn.sigmoid(gate)
        for t in range(GLU_TILES):
            for r in range(tm // CONV_ROWS):
                _conv_unit(ext_ref, convw_ref, y_ref, r * CONV_ROWS, piece * GLU_TILES + t)
    q = _dot(h, win_ref[:, pl.ds(2 * CONV_CH, 2 * SGU_CH)])
    for c in range(n_tiles):
        tail = ext_ref[c, pl.ds(EXT_PITCH * tm, HALO, stride=EXT_PITCH), :]
        ext_ref[c, pl.ds(0, HALO, stride=EXT_PITCH), :] = tail

    y = y_ref[...] + convb_ref[...]
    y = _silu(_layer_norm(y, clng_ref[...], clnb_ref[...]))
    vb_ref[:, pl.ds(0, CONV_CH)] = _rms(y, onc_ref[...]).astype(BF16)

    u = _gelu(q[:, :SGU_CH])
    v = _layer_norm(_gelu(q[:, SGU_CH:]), slng_ref[...], slnb_ref[...]).astype(BF16)
    blk = lax.broadcasted_iota(jnp.int32, (GMLP_CHUNK, GMLP_CHUNK), 0) // CHUNK
    blk_t = lax.broadcasted_iota(jnp.int32, (GMLP_CHUNK, GMLP_CHUNK), 1) // CHUNK
    mask = blk_t <= blk
    ws = [jnp.where(mask, sw_ref[hd], 0.0).astype(BF16) for hd in range(SGU_GROUPS)]
    for n in range(tm // GMLP_CHUNK):
        r0 = n * GMLP_CHUNK
        tiles = []
        for hd in range(SGU_GROUPS):
            c0 = hd * SGU_GROUP_DIM
            mixed = _dot(ws[hd], v[r0:r0 + GMLP_CHUNK, c0:c0 + SGU_GROUP_DIM])
            mixed = mixed + sbias_ref[:, pl.ds(c0, SGU_GROUP_DIM)]
            tiles.append(u[r0:r0 + GMLP_CHUNK, c0:c0 + SGU_GROUP_DIM] * mixed)
        gated = jnp.concatenate(tiles, axis=1)
        vb_ref[pl.ds(r0, GMLP_CHUNK), pl.ds(CONV_CH, SGU_CH)] = _rms(gated, ons_ref[...]).astype(BF16)

    o_ref[...] = x_ref[...] + _dot(vb_ref[...], wout_ref[...])


def _mix(x, norm_g, w_in, w_out, conv_w, conv_b, cln_g, cln_b, sln_g, sln_b, sgu_w, sgu_bias,
         on_conv, on_sgu, *, seq, cast_to_bf16=()):
    t, d = x.shape
    tm = MIX_TM
    n_steps = t // tm
    args = (norm_g, w_in, w_out, conv_w, conv_b, cln_g, cln_b, sln_g, sln_b, sgu_w, sgu_bias,
            on_conv, on_sgu)
    side = list(cast_to_bf16)

    def slab_spec(w):
        assert w.shape[0] % n_steps == 0
        return pl.BlockSpec((w.shape[0] // n_steps, w.shape[1]), lambda i: (i, 0))

    side_specs = [slab_spec(w) for w in side]
    outs = pl.pallas_call(
        functools.partial(_mix_kernel, blocks_per_seq=seq // tm, n_side=len(side)),
        out_shape=[jax.ShapeDtypeStruct((t, d), F32)] + [jax.ShapeDtypeStruct(w.shape, BF16) for w in side],
        grid=(n_steps,),
        in_specs=[pl.BlockSpec((tm, d), lambda i: (i, 0))] + [_resident(a) for a in args] + side_specs,
        out_specs=[pl.BlockSpec((tm, d), lambda i: (i, 0))] + side_specs,
        scratch_shapes=[
            pltpu.VMEM((CONV_CH // LANES, EXT_PITCH * (HALO + tm), LANES), F32),
            pltpu.VMEM((tm, CONV_CH), F32),
            pltpu.VMEM((tm, CONV_CH + SGU_CH), BF16),
        ],
        compiler_params=pltpu.CompilerParams(
            dimension_semantics=("arbitrary",), vmem_limit_bytes=VMEM_LIMIT),
        name="mix",
    )(x, *args, *side)
    return outs[0], outs[1:]


def _kv_kernel(m_ref, g_ref, w_ref, o_ref, h_ref):
    @pl.when(pl.program_id(0) == 0)
    def _():
        h_ref[...] = _rms(m_ref[...], g_ref[...]).astype(BF16)
    o_ref[...] = _dot(h_ref[...], w_ref[...].astype(BF16)).astype(BF16)


def _kv(mem, norm_g, w_kv):
    m, d = mem.shape
    n = w_kv.shape[1]
    return pl.pallas_call(
        _kv_kernel,
        out_shape=jax.ShapeDtypeStruct((m, n), BF16),
        grid=(n // KV_TN,),
        in_specs=[
            pl.BlockSpec((m, d), lambda j: (0, 0)),
            pl.BlockSpec((1, d), lambda j: (0, 0)),
            pl.BlockSpec((d, KV_TN), lambda j: (0, j)),
        ],
        out_specs=pl.BlockSpec((m, KV_TN), lambda j: (0, j)),
        scratch_shapes=[pltpu.VMEM((m, d), BF16)],
        compiler_params=pltpu.CompilerParams(
            dimension_semantics=("arbitrary",), vmem_limit_bytes=VMEM_LIMIT),
        name="kv",
    )(mem, norm_g, w_kv)


def _attn_kernel(x_ref, g_ref, wq_ref, k_ref, v_ref, wo_ref, o_ref, oh_ref):
    h = _rms(x_ref[...], g_ref[...]).astype(BF16)
    q = _dot(h, wq_ref[...]).astype(BF16)
    for hd in range(XA_HEADS):
        cols = pl.ds(hd * XA_HEAD_DIM, XA_HEAD_DIM)
        qh = q[:, hd * XA_HEAD_DIM:(hd + 1) * XA_HEAD_DIM]
        s = lax.dot_general(qh, k_ref[0, :, cols], (((1,), (1,)), ((), ())), preferred_element_type=F32)
        s = s * (XA_HEAD_DIM ** -0.5)
        e = jnp.exp(s - jnp.max(s, axis=-1, keepdims=True))
        p = (e / jnp.sum(e, axis=-1, keepdims=True)).astype(BF16)
        oh_ref[:, cols] = _dot(p, v_ref[0, :, cols]).astype(BF16)
    o_ref[...] = x_ref[...] + _dot(oh_ref[...], wo_ref[...])


def _attn(x, norm_g, w_q, kv, w_o, *, seq):
    t, d = x.shape
    tm = ATT_TM
    bps = seq // tm
    n_mem = kv.shape[1]
    return pl.pallas_call(
        _attn_kernel,
        out_shape=jax.ShapeDtypeStruct((t, d), F32),
        grid=(t // tm,),
        in_specs=[
            pl.BlockSpec((tm, d), lambda i: (i, 0)),
            _resident(norm_g),
            _resident(w_q),
            pl.BlockSpec((1, n_mem, d), lambda i: (i // bps, 0, 0)),
            pl.BlockSpec((1, n_mem, d), lambda i: (i // bps, 0, 1)),
            _resident(w_o),
        ],
        out_specs=pl.BlockSpec((tm, d), lambda i: (i, 0)),
        scratch_shapes=[pltpu.VMEM((tm, d), BF16)],
        compiler_params=pltpu.CompilerParams(
            dimension_semantics=("arbitrary",), vmem_limit_bytes=VMEM_LIMIT),
        name="attn",
    )(x, norm_g, w_q, kv, kv, w_o)


def kernel(x, mem, ffn1_norm, ffn1_w_in, ffn1_w_out, mix_norm, w_mix_in, conv_w, conv_b, conv_ln_g, conv_ln_b, sgu_ln_g, sgu_ln_b, sgu_w, sgu_b, out_norm_conv, out_norm_sgu, w_mix_out, xattn_norm, mem_norm, w_q, w_kv, w_o, ffn2_norm, ffn2_w_in, ffn2_w_out, final_norm):
    b, s, d = x.shape
    n_mem = mem.shape[1]
    depth = ffn1_norm.shape[0]
    row = lambda v: v.reshape(1, -1)
    ones = jnp.ones((1, d), F32)

    xs = x.reshape(b * s, d)
    mem2 = mem.reshape(b * n_mem, d)
    for l in range(depth):
        last = l == depth - 1
        xs, (w_mix_in_b, w_mix_out_b) = _ffn(
            xs, row(ffn1_norm[l]), ffn1_w_in[l], ffn1_w_out[l], ones, final_norm=False,
            cast_to_bf16=((w_mix_in[l], _glu_paired_lane_tiles()), (w_mix_out[l], None)))
        sgu_bias = jnp.repeat(jnp.transpose(sgu_b[l]), SGU_GROUP_DIM, axis=1)
        xs, (w_q_b, w_o_b) = _mix(
            xs, row(mix_norm[l]), w_mix_in_b, w_mix_out_b,
            conv_w[l], row(conv_b[l]), row(conv_ln_g[l]), row(conv_ln_b[l]),
            row(sgu_ln_g[l]), row(sgu_ln_b[l]), sgu_w[l], sgu_bias,
            row(out_norm_conv[l]), row(out_norm_sgu[l]), seq=s, cast_to_bf16=(w_q[l], w_o[l]))
        kv = _kv(mem2, row(mem_norm[l]), w_kv[l]).reshape(b, n_mem, 2 * d)
        xs = _attn(xs, row(xattn_norm[l]), w_q_b, kv, w_o_b, seq=s)
        xs, _ = _ffn(xs, row(ffn2_norm[l]), ffn2_w_in[l], ffn2_w_out[l],
                     row(final_norm) if last else ones, final_norm=last)
    return xs.reshape(b, s, d)
```

```python
import functools

import jax
import jax.numpy as jnp
from jax import lax
from jax.experimental import pallas as pl
from jax.experimental.pallas import tpu as pltpu

D_MODEL = 2048
D_FF = 5632
CONV_CH = 1024
CONV_WIDTH = 31
SGU_CH = 1024
SGU_GROUPS = 8
SGU_GROUP_DIM = SGU_CH // SGU_GROUPS
GMLP_CHUNK = 128
CHUNK = 64
XA_HEADS = 4
XA_HEAD_DIM = D_MODEL // XA_HEADS
MACARON_SCALE = 0.5
RMS_EPS = 1e-6
LN_EPS = 1e-5

F32 = jnp.float32
BF16 = jnp.bfloat16

LANES = 128
SUBLANES = 8
HALO = 32
VMEM_LIMIT = 62 * 1024 * 1024

FFN_TM, FFN_TF = 1024, 512
MIX_TM = 512
ATT_TM = 512
KV_TN = 1024
CONV_ROWS = 64
EXT_PITCH = 2
GLU_TILES = 2
ROW_CHUNK = 128
CAST_ROWS = 32


def _rms(x, g):
    ms = jnp.mean(x * x, axis=-1, keepdims=True)
    return x * lax.rsqrt(ms + RMS_EPS) * g


def _layer_norm(x, g, b):
    mu = jnp.mean(x, axis=-1, keepdims=True)
    xc = x - mu
    var = jnp.mean(xc * xc, axis=-1, keepdims=True)
    return xc * lax.rsqrt(var + LN_EPS) * g + b


def _silu(x):
    return x * jax.nn.sigmoid(x)


def _gelu(x):
    return 0.5 * x * (1.0 + lax.erf(x * (0.5 ** 0.5)))


def _dot(a, b):
    return jnp.dot(a, b, preferred_element_type=F32)


def _for_row_chunks(n_rows, chunk, body):
    def step(r, carry):
        body(pl.ds(pl.multiple_of(r * chunk, chunk), chunk))
        return carry
    lax.fori_loop(0, n_rows // chunk, step, 0)


def _resident(a):
    return pl.BlockSpec(a.shape, lambda *_: (0,) * a.ndim, pipeline_mode=pl.Buffered(1))


def _cast_slabs(srcs, dsts, orders=None):
    for k, (src, dst) in enumerate(zip(srcs, dsts)):
        order = None if orders is None else orders[k]
        if order is None:
            dst[...] = src[...].astype(BF16)
        else:
            for to_tile, from_tile in enumerate(order):
                dst[:, pl.ds(to_tile * LANES, LANES)] = src[:, pl.ds(from_tile * LANES, LANES)].astype(BF16)


def _ffn_kernel(x_hbm, g_ref, wg_ref, wu_ref, wo_ref, fin_ref, *rest, final_norm, side_orders):
    n_side = len(side_orders)
    side_in, o_ref, side_out = rest[:n_side], rest[n_side], rest[n_side + 1:2 * n_side + 1]
    h_ref, a_ref, xbuf, xsem = rest[2 * n_side + 1:]
    i = pl.program_id(0)
    j = pl.program_id(1)
    n_tiles = pl.num_programs(1) - 1
    tm = o_ref.shape[0]

    def x_copy(block):
        rows = pl.ds(pl.multiple_of(block * tm, tm), tm)
        return pltpu.make_async_copy(x_hbm.at[rows], xbuf, xsem.at[0])

    @pl.when((i == 0) & (j == 0))
    def _():
        x_copy(0).start()

    def project_out():
        return _dot(a_ref[...], wo_ref[...].astype(BF16))

    def activate(h):
        gate = _dot(h, wg_ref[...].astype(BF16))
        up = _dot(h, wu_ref[...].astype(BF16))
        a_ref[...] = (MACARON_SCALE * _silu(gate) * up).astype(BF16)
        _cast_slabs(side_in, side_out, side_orders)

    @pl.when(j == 0)
    def _():
        x_copy(i).wait()
        x = xbuf[...]
        h = _rms(x, g_ref[...]).astype(BF16)
        h_ref[...] = h
        o_ref[...] = x
        activate(h)

    @pl.when((j == 1) & (i + 1 < pl.num_programs(0)))
    def _():
        x_copy(i + 1).start()

    @pl.when((j > 0) & (j < n_tiles))
    def _():
        o_ref[...] += project_out()
        activate(h_ref[...])

    @pl.when(j == n_tiles)
    def _():
        o_ref[...] += project_out()
        _cast_slabs(side_in, side_out, side_orders)

    if final_norm:
        @pl.when(j == n_tiles)
        def _():
            def norm_rows(rows):
                o_ref[rows, :] = _rms(o_ref[rows, :], fin_ref[...])
            _for_row_chunks(tm, ROW_CHUNK, norm_rows)


def _ffn(x, norm_g, w_in, w_out, fin_g, *, final_norm, cast_to_bf16=()):
    t, d = x.shape
    f = w_out.shape[0]
    tm, tf = FFN_TM, FFN_TF
    nf = f // tf
    steps_per_block = nf + 1
    n_steps = (t // tm) * steps_per_block
    assert nf >= 2, "the next row block is prefetched at step 1"
    side = [w for w, _ in cast_to_bf16]
    side_orders = tuple(None if order is None else tuple(order) for _, order in cast_to_bf16)

    def slab_spec(w):
        n_slabs = w.shape[0] // CAST_ROWS
        assert w.shape[0] % CAST_ROWS == 0 and n_slabs <= n_steps
        return pl.BlockSpec((CAST_ROWS, w.shape[1]),
                            lambda i, j: (jnp.minimum(i * steps_per_block + j, n_slabs - 1), 0))

    side_specs = [slab_spec(w) for w in side]
    outs = pl.pallas_call(
        functools.partial(_ffn_kernel, final_norm=final_norm, side_orders=side_orders),
        out_shape=[jax.ShapeDtypeStruct((t, d), F32)] + [jax.ShapeDtypeStruct(w.shape, BF16) for w in side],
        grid=(t // tm, steps_per_block),
        in_specs=[
            pl.BlockSpec(memory_space=pl.ANY),
            pl.BlockSpec((1, d), lambda i, j: (0, 0)),
            pl.BlockSpec((d, tf), lambda i, j: (0, jnp.minimum(j, nf - 1))),
            pl.BlockSpec((d, tf), lambda i, j: (0, jnp.minimum(j, nf - 1) + nf)),
            pl.BlockSpec((tf, d), lambda i, j: (jnp.maximum(j - 1, 0), 0)),
            pl.BlockSpec((1, d), lambda i, j: (0, 0)),
        ] + side_specs,
        out_specs=[pl.BlockSpec((tm, d), lambda i, j: (i, 0))] + side_specs,
        scratch_shapes=[
            pltpu.VMEM((tm, d), BF16),
            pltpu.VMEM((tm, tf), BF16),
            pltpu.VMEM((tm, d), F32),
            pltpu.SemaphoreType.DMA((1,)),
        ],
        compiler_params=pltpu.CompilerParams(
            dimension_semantics=("arbitrary", "arbitrary"), vmem_limit_bytes=VMEM_LIMIT),
        name="ffn_final" if final_norm else "ffn",
    )(x, norm_g, w_in, w_in, w_out, fin_g, *side)
    return outs[0], outs[1:]


def _glu_paired_lane_tiles():
    n = CONV_CH // LANES
    order = []
    for c in range(n):
        order += [c, n + c]
    return order + list(range(2 * n, 2 * n + 2 * SGU_CH // LANES))


def _conv_unit(ext_ref, w_ref, y_ref, row0, c):
    n = CONV_ROWS
    first = HALO - (CONV_WIDTH - 1)
    lanes = pl.ds(c * LANES, LANES)
    acc = jnp.zeros((n, LANES), F32)
    for k in range(CONV_WIDTH):
        window = ext_ref[c, pl.ds(EXT_PITCH * (row0 + first + k), n, stride=EXT_PITCH), :]
        acc = acc + w_ref[pl.ds(k, 1), lanes] * window
    y_ref[pl.ds(row0, n), lanes] = acc


def _mix_kernel(x_ref, g_ref, win_ref, wout_ref, convw_ref, convb_ref, clng_ref, clnb_ref,
                slng_ref, slnb_ref, sw_ref, sbias_ref, onc_ref, ons_ref, *rest, blocks_per_seq, n_side):
    side_in, o_ref, side_out = rest[:n_side], rest[n_side], rest[n_side + 1:2 * n_side + 1]
    ext_ref, y_ref, vb_ref = rest[2 * n_side + 1:]
    i = pl.program_id(0)
    tm = o_ref.shape[0]
    n_tiles = CONV_CH // LANES
    halo_rows = EXT_PITCH * HALO

    @pl.when(i % blocks_per_seq == 0)
    def _():
        ext_ref[:, pl.ds(0, halo_rows), :] = jnp.zeros((n_tiles, halo_rows, LANES), F32)

    _cast_slabs(side_in, side_out)
    h = _rms(x_ref[...], g_ref[...]).astype(BF16)

    slice_cols = 2 * LANES * GLU_TILES
    for piece in range(n_tiles // GLU_TILES):
        p = _dot(h, win_ref[:, pl.ds(piece * slice_cols, slice_cols)])
        for t in range(GLU_TILES):
            val = p[:, 2 * LANES * t:2 * LANES * t + LANES]
            gate = p[:, 2 * LANES * t + LANES:2 * LANES * (t + 1)]
            ext_ref[piece * GLU_TILES + t, pl.ds(halo_rows, tm, stride=EXT_PITCH), :] = val * jax.nn.sigmoid(gate)
        for t in range(GLU_TILES):
            for r in range(tm // CONV_ROWS):
                _conv_unit(ext_ref, convw_ref, y_ref, r * CONV_ROWS, piece * GLU_TILES + t)
    q = _dot(h, win_ref[:, pl.ds(2 * CONV_CH, 2 * SGU_CH)])
    for c in range(n_tiles):
        tail = ext_ref[c, pl.ds(EXT_PITCH * tm, HALO, stride=EXT_PITCH), :]
        ext_ref[c, pl.ds(0, HALO, stride=EXT_PITCH), :] = tail

    y = y_ref[...] + convb_ref[...]
    y = _silu(_layer_norm(y, clng_ref[...], clnb_ref[...]))
    vb_ref[:, pl.ds(0, CONV_CH)] = _rms(y, onc_ref[...]).astype(BF16)

    u = _gelu(q[:, :SGU_CH])
    v = _layer_norm(_gelu(q[:, SGU_CH:]), slng_ref[...], slnb_ref[...]).astype(BF16)
    blk = lax.broadcasted_iota(jnp.int32, (GMLP_CHUNK, GMLP_CHUNK), 0) // CHUNK
    blk_t = lax.broadcasted_iota(jnp.int32, (GMLP_CHUNK, GMLP_CHUNK), 1) // CHUNK
    mask = blk_t <= blk
    ws = [jnp.where(mask, sw_ref[hd], 0.0).astype(BF16) for hd in range(SGU_GROUPS)]
    for n in range(tm // GMLP_CHUNK):
        r0 = n * GMLP_CHUNK
        tiles = []
        for hd in range(SGU_GROUPS):
            c0 = hd * SGU_GROUP_DIM
            mixed = _dot(ws[hd], v[r0:r0 + GMLP_CHUNK, c0:c0 + SGU_GROUP_DIM])
            mixed = mixed + sbias_ref[:, pl.ds(c0, SGU_GROUP_DIM)]
            tiles.append(u[r0:r0 + GMLP_CHUNK, c0:c0 + SGU_GROUP_DIM] * mixed)
        gated = jnp.concatenate(tiles, axis=1)
        vb_ref[pl.ds(r0, GMLP_CHUNK), pl.ds(CONV_CH, SGU_CH)] = _rms(gated, ons_ref[...]).astype(BF16)

    o_ref[...] = x_ref[...] + _dot(vb_ref[...], wout_ref[...])


def _mix(x, norm_g, w_in, w_out, conv_w, conv_b, cln_g, cln_b, sln_g, sln_b, sgu_w, sgu_bias,
         on_conv, on_sgu, *, seq, cast_to_bf16=()):
    t, d = x.shape
    tm = MIX_TM
    n_steps = t // tm
    args = (norm_g, w_in, w_out, conv_w, conv_b, cln_g, cln_b, sln_g, sln_b, sgu_w, sgu_bias,
            on_conv, on_sgu)
    side = list(cast_to_bf16)

    def slab_spec(w):
        assert w.shape[0] % n_steps == 0
        return pl.BlockSpec((w.shape[0] // n_steps, w.shape[1]), lambda i: (i, 0))

    side_specs = [slab_spec(w) for w in side]
    outs = pl.pallas_call(
        functools.partial(_mix_kernel, blocks_per_seq=seq // tm, n_side=len(side)),
        out_shape=[jax.ShapeDtypeStruct((t, d), F32)] + [jax.ShapeDtypeStruct(w.shape, BF16) for w in side],
        grid=(n_steps,),
        in_specs=[pl.BlockSpec((tm, d), lambda i: (i, 0))] + [_resident(a) for a in args] + side_specs,
        out_specs=[pl.BlockSpec((tm, d), lambda i: (i, 0))] + side_specs,
        scratch_shapes=[
            pltpu.VMEM((CONV_CH // LANES, EXT_PITCH * (HALO + tm), LANES), F32),
            pltpu.VMEM((tm, CONV_CH), F32),
            pltpu.VMEM((tm, CONV_CH + SGU_CH), BF16),
        ],
        compiler_params=pltpu.CompilerParams(
            dimension_semantics=("arbitrary",), vmem_limit_bytes=VMEM_LIMIT),
        name="mix",
    )(x, *args, *side)
    return outs[0], outs[1:]


def _kv_kernel(m_ref, g_ref, w_ref, o_ref, h_ref):
    @pl.when(pl.program_id(0) == 0)
    def _():
        h_ref[...] = _rms(m_ref[...], g_ref[...]).astype(BF16)
    o_ref[...] = _dot(h_ref[...], w_ref[...].astype(BF16)).astype(BF16)


def _kv(mem, norm_g, w_kv):
    m, d = mem.shape
    n = w_kv.shape[1]
    return pl.pallas_call(
        _kv_kernel,
        out_shape=jax.ShapeDtypeStruct((m, n), BF16),
        grid=(n // KV_TN,),
        in_specs=[
            pl.BlockSpec((m, d), lambda j: (0, 0)),
            pl.BlockSpec((1, d), lambda j: (0, 0)),
            pl.BlockSpec((d, KV_TN), lambda j: (0, j)),
        ],
        out_specs=pl.BlockSpec((m, KV_TN), lambda j: (0, j)),
        scratch_shapes=[pltpu.VMEM((m, d), BF16)],
        compiler_params=pltpu.CompilerParams(
            dimension_semantics=("arbitrary",), vmem_limit_bytes=VMEM_LIMIT),
        name="kv",
    )(mem, norm_g, w_kv)


def _attn_kernel(x_ref, g_ref, wq_ref, k_ref, v_ref, wo_ref, o_ref, oh_ref):
    h = _rms(x_ref[...], g_ref[...]).astype(BF16)
    q = _dot(h, wq_ref[...]).astype(BF16)
    for hd in range(XA_HEADS):
        cols = pl.ds(hd * XA_HEAD_DIM, XA_HEAD_DIM)
        qh = q[:, hd * XA_HEAD_DIM:(hd + 1) * XA_HEAD_DIM]
        s = lax.dot_general(qh, k_ref[0, :, cols], (((1,), (1,)), ((), ())), preferred_element_type=F32)
        s = s * (XA_HEAD_DIM ** -0.5)
        e = jnp.exp(s - jnp.max(s, axis=-1, keepdims=True))
        p = (e / jnp.sum(e, axis=-1, keepdims=True)).astype(BF16)
        oh_ref[:, cols] = _dot(p, v_ref[0, :, cols]).astype(BF16)
    o_ref[...] = x_ref[...] + _dot(oh_ref[...], wo_ref[...])


def _attn(x, norm_g, w_q, kv, w_o, *, seq):
    t, d = x.shape
    tm = ATT_TM
    bps = seq // tm
    n_mem = kv.shape[1]
    return pl.pallas_call(
        _attn_kernel,
        out_shape=jax.ShapeDtypeStruct((t, d), F32),
        grid=(t // tm,),
        in_specs=[
            pl.BlockSpec((tm, d), lambda i: (i, 0)),
            _resident(norm_g),
            _resident(w_q),
            pl.BlockSpec((1, n_mem, d), lambda i: (i // bps, 0, 0)),
            pl.BlockSpec((1, n_mem, d), lambda i: (i // bps, 0, 1)),
            _resident(w_o),
        ],
        out_specs=pl.BlockSpec((tm, d), lambda i: (i, 0)),
        scratch_shapes=[pltpu.VMEM((tm, d), BF16)],
        compiler_params=pltpu.CompilerParams(
            dimension_semantics=("arbitrary",), vmem_limit_bytes=VMEM_LIMIT),
        name="attn",
    )(x, norm_g, w_q, kv, kv, w_o)


def kernel(x, mem, ffn1_norm, ffn1_w_in, ffn1_w_out, mix_norm, w_mix_in, conv_w, conv_b, conv_ln_g, conv_ln_b, sgu_ln_g, sgu_ln_b, sgu_w, sgu_b, out_norm_conv, out_norm_sgu, w_mix_out, xattn_norm, mem_norm, w_q, w_kv, w_o, ffn2_norm, ffn2_w_in, ffn2_w_out, final_norm):
    b, s, d = x.shape
    n_mem = mem.shape[1]
    depth = ffn1_norm.shape[0]
    row = lambda v: v.reshape(1, -1)
    ones = jnp.ones((1, d), F32)

    xs = x.reshape(b * s, d)
    mem2 = mem.reshape(b * n_mem, d)
    for l in range(depth):
        last = l == depth - 1
        xs, (w_mix_in_b, w_mix_out_b) = _ffn(
            xs, row(ffn1_norm[l]), ffn1_w_in[l], ffn1_w_out[l], ones, final_norm=False,
            cast_to_bf16=((w_mix_in[l], _glu_paired_lane_tiles()), (w_mix_out[l], None)))
        sgu_bias = jnp.repeat(jnp.transpose(sgu_b[l]), SGU_GROUP_DIM, axis=1)
        xs, (w_q_b, w_o_b) = _mix(
            xs, row(mix_norm[l]), w_mix_in_b, w_mix_out_b,
            conv_w[l], row(conv_b[l]), row(conv_ln_g[l]), row(conv_ln_b[l]),
            row(sgu_ln_g[l]), row(sgu_ln_b[l]), sgu_w[l], sgu_bias,
            row(out_norm_conv[l]), row(out_norm_sgu[l]), seq=s, cast_to_bf16=(w_q[l], w_o[l]))
        kv = _kv(mem2, row(mem_norm[l]), w_kv[l]).reshape(b, n_mem, 2 * d)
        xs = _attn(xs, row(xattn_norm[l]), w_q_b, kv, w_o_b, seq=s)
        xs, _ = _ffn(xs, row(ffn2_norm[l]), ffn2_w_in[l], ffn2_w_out[l],
                     row(final_norm) if last else ones, final_norm=last)
    return xs.reshape(b, s, d)
```

```python
import functools

import jax
import jax.numpy as jnp
from jax import lax
from jax.experimental import pallas as pl
from jax.experimental.pallas import tpu as pltpu

D_MODEL = 2048
D_FF = 5632
CONV_CH = 1024
CONV_WIDTH = 31
SGU_CH = 1024
SGU_GROUPS = 8
SGU_GROUP_DIM = SGU_CH // SGU_GROUPS
GMLP_CHUNK = 128
CHUNK = 64
XA_HEADS = 4
XA_HEAD_DIM = D_MODEL // XA_HEADS
MACARON_SCALE = 0.5
RMS_EPS = 1e-6
LN_EPS = 1e-5

F32 = jnp.float32
BF16 = jnp.bfloat16

LANES = 128
SUBLANES = 8
HALO = 32
VMEM_LIMIT = 62 * 1024 * 1024

FFN_TM, FFN_TF = 1024, 512
MIX_TM = 512
ATT_TM = 512
KV_TN = 1024
CONV_ROWS = 64
EXT_PITCH = 2
GLU_TILES = 2
ROW_CHUNK = 128
CAST_ROWS = 32


def _rms(x, g):
    ms = jnp.mean(x * x, axis=-1, keepdims=True)
    return x * lax.rsqrt(ms + RMS_EPS) * g


def _layer_norm(x, g, b):
    mu = jnp.mean(x, axis=-1, keepdims=True)
    xc = x - mu
    var = jnp.mean(xc * xc, axis=-1, keepdims=True)
    return xc * lax.rsqrt(var + LN_EPS) * g + b


def _silu(x):
    return x * jax.nn.sigmoid(x)


def _gelu(x):
    return 0.5 * x * (1.0 + lax.erf(x * (0.5 ** 0.5)))


def _dot(a, b):
    return jnp.dot(a, b, preferred_element_type=F32)


def _for_row_chunks(n_rows, chunk, body):
    def step(r, carry):
        body(pl.ds(pl.multiple_of(r * chunk, chunk), chunk))
        return carry
    lax.fori_loop(0, n_rows // chunk, step, 0)


def _resident(a):
    return pl.BlockSpec(a.shape, lambda *_: (0,) * a.ndim, pipeline_mode=pl.Buffered(1))


def _cast_slabs(srcs, dsts, orders=None):
    for k, (src, dst) in enumerate(zip(srcs, dsts)):
        order = None if orders is None else orders[k]
        if order is None:
            dst[...] = src[...].astype(BF16)
        else:
            for to_tile, from_tile in enumerate(order):
                dst[:, pl.ds(to_tile * LANES, LANES)] = src[:, pl.ds(from_tile * LANES, LANES)].astype(BF16)


def _ffn_kernel(x_hbm, g_ref, wg_ref, wu_ref, wo_ref, fin_ref, *rest, final_norm, side_orders):
    n_side = len(side_orders)
    side_in, o_ref, side_out = rest[:n_side], rest[n_side], rest[n_side + 1:2 * n_side + 1]
    h_ref, xbuf, xsem = rest[2 * n_side + 1:]
    i = pl.program_id(0)
    j = pl.program_id(1)
    tm = o_ref.shape[0]

    def x_copy(block):
        rows = pl.ds(pl.multiple_of(block * tm, tm), tm)
        return pltpu.make_async_copy(x_hbm.at[rows], xbuf, xsem.at[0])

    @pl.when((i == 0) & (j == 0))
    def _():
        x_copy(0).start()

    def hidden_tile(h):
        gate = _dot(h, wg_ref[...].astype(BF16))
        up = _dot(h, wu_ref[...].astype(BF16))
        a = (MACARON_SCALE * _silu(gate) * up).astype(BF16)
        out = _dot(a, wo_ref[...].astype(BF16))
        _cast_slabs(side_in, side_out, side_orders)
        return out

    @pl.when(j == 0)
    def _():
        x_copy(i).wait()
        x = xbuf[...]
        h = _rms(x, g_ref[...]).astype(BF16)
        h_ref[...] = h
        o_ref[...] = x + hidden_tile(h)

    @pl.when((j == 1) & (i + 1 < pl.num_programs(0)))
    def _():
        x_copy(i + 1).start()

    @pl.when(j > 0)
    def _():
        o_ref[...] += hidden_tile(h_ref[...])

    if final_norm:
        @pl.when(j == pl.num_programs(1) - 1)
        def _():
            def norm_rows(rows):
                o_ref[rows, :] = _rms(o_ref[rows, :], fin_ref[...])
            _for_row_chunks(tm, ROW_CHUNK, norm_rows)


def _ffn(x, norm_g, w_in, w_out, fin_g, *, final_norm, cast_to_bf16=()):
    t, d = x.shape
    f = w_out.shape[0]
    tm, tf = FFN_TM, FFN_TF
    nf = f // tf
    n_steps = (t // tm) * nf
    assert nf >= 2, "the next row block is prefetched at hidden tile 1"
    side = [w for w, _ in cast_to_bf16]
    side_orders = tuple(None if order is None else tuple(order) for _, order in cast_to_bf16)

    def slab_spec(w):
        n_slabs = w.shape[0] // CAST_ROWS
        assert w.shape[0] % CAST_ROWS == 0 and n_slabs <= n_steps
        return pl.BlockSpec((CAST_ROWS, w.shape[1]), lambda i, j: (jnp.minimum(i * nf + j, n_slabs - 1), 0))

    side_specs = [slab_spec(w) for w in side]
    outs = pl.pallas_call(
        functools.partial(_ffn_kernel, final_norm=final_norm, side_orders=side_orders),
        out_shape=[jax.ShapeDtypeStruct((t, d), F32)] + [jax.ShapeDtypeStruct(w.shape, BF16) for w in side],
        grid=(t // tm, nf),
        in_specs=[
            pl.BlockSpec(memory_space=pl.ANY),
            pl.BlockSpec((1, d), lambda i, j: (0, 0)),
            pl.BlockSpec((d, tf), lambda i, j: (0, j)),
            pl.BlockSpec((d, tf), lambda i, j: (0, j + nf)),
            pl.BlockSpec((tf, d), lambda i, j: (j, 0)),
            pl.BlockSpec((1, d), lambda i, j: (0, 0)),
        ] + side_specs,
        out_specs=[pl.BlockSpec((tm, d), lambda i, j: (i, 0))] + side_specs,
        scratch_shapes=[
            pltpu.VMEM((tm, d), BF16),
            pltpu.VMEM((tm, d), F32),
            pltpu.SemaphoreType.DMA((1,)),
        ],
        compiler_params=pltpu.CompilerParams(
            dimension_semantics=("arbitrary", "arbitrary"), vmem_limit_bytes=VMEM_LIMIT),
        name="ffn_final" if final_norm else "ffn",
    )(x, norm_g, w_in, w_in, w_out, fin_g, *side)
    return outs[0], outs[1:]


def _glu_paired_lane_tiles():
    n = CONV_CH // LANES
    order = []
    for c in range(n):
        order += [c, n + c]
    return order + list(range(2 * n, 2 * n + 2 * SGU_CH // LANES))


def _conv_unit(ext_ref, w_ref, y_ref, row0, c):
    n = CONV_ROWS
    first = HALO - (CONV_WIDTH - 1)
    lanes = pl.ds(c * LANES, LANES)
    acc = jnp.zeros((n, LANES), F32)
    for k in range(CONV_WIDTH):
        window = ext_ref[c, pl.ds(EXT_PITCH * (row0 + first + k), n, stride=EXT_PITCH), :]
        acc = acc + w_ref[pl.ds(k, 1), lanes] * window
    y_ref[pl.ds(row0, n), lanes] = acc


def _mix_kernel(x_ref, g_ref, win_ref, wout_ref, convw_ref, convb_ref, clng_ref, clnb_ref,
                slng_ref, slnb_ref, sw_ref, sbias_ref, onc_ref, ons_ref, *rest, blocks_per_seq, n_side):
    side_in, o_ref, side_out = rest[:n_side], rest[n_side], rest[n_side + 1:2 * n_side + 1]
    ext_ref, y_ref, vb_ref = rest[2 * n_side + 1:]
    i = pl.program_id(0)
    tm = o_ref.shape[0]
    n_tiles = CONV_CH // LANES
    halo_rows = EXT_PITCH * HALO

    @pl.when(i % blocks_per_seq == 0)
    def _():
        ext_ref[:, pl.ds(0, halo_rows), :] = jnp.zeros((n_tiles, halo_rows, LANES), F32)

    _cast_slabs(side_in, side_out)
    h = _rms(x_ref[...], g_ref[...]).astype(BF16)

    slice_cols = 2 * LANES * GLU_TILES
    for piece in range(n_tiles // GLU_TILES):
        p = _dot(h, win_ref[:, pl.ds(piece * slice_cols, slice_cols)])
        for t in range(GLU_TILES):
            val = p[:, 2 * LANES * t:2 * LANES * t + LANES]
            gate = p[:, 2 * LANES * t + LANES:2 * LANES * (t + 1)]
            ext_ref[piece * GLU_TILES + t, pl.ds(halo_rows, tm, stride=EXT_PITCH), :] = val * jax.nn.sigmoid(gate)
        for t in range(GLU_TILES):
            for r in range(tm // CONV_ROWS):
                _conv_unit(ext_ref, convw_ref, y_ref, r * CONV_ROWS, piece * GLU_TILES + t)
    q = _dot(h, win_ref[:, pl.ds(2 * CONV_CH, 2 * SGU_CH)])
    for c in range(n_tiles):
        tail = ext_ref[c, pl.ds(EXT_PITCH * tm, HALO, stride=EXT_PITCH), :]
        ext_ref[c, pl.ds(0, HALO, stride=EXT_PITCH), :] = tail

    y = y_ref[...] + convb_ref[...]
    y = _silu(_layer_norm(y, clng_ref[...], clnb_ref[...]))
    vb_ref[:, pl.ds(0, CONV_CH)] = _rms(y, onc_ref[...]).astype(BF16)

    u = _gelu(q[:, :SGU_CH])
    v = _layer_norm(_gelu(q[:, SGU_CH:]), slng_ref[...], slnb_ref[...]).astype(BF16)
    blk = lax.broadcasted_iota(jnp.int32, (GMLP_CHUNK, GMLP_CHUNK), 0) // CHUNK
    blk_t = lax.broadcasted_iota(jnp.int32, (GMLP_CHUNK, GMLP_CHUNK), 1) // CHUNK
    mask = blk_t <= blk
    ws = [jnp.where(mask, sw_ref[hd], 0.0).astype(BF16) for hd in range(SGU_GROUPS)]
    for n in range(tm // GMLP_CHUNK):
        r0 = n * GMLP_CHUNK
        tiles = []
        for hd in range(SGU_GROUPS):
            c0 = hd * SGU_GROUP_DIM
            mixed = _dot(ws[hd], v[r0:r0 + GMLP_CHUNK, c0:c0 + SGU_GROUP_DIM])
            mixed = mixed + sbias_ref[:, pl.ds(c0, SGU_GROUP_DIM)]
            tiles.append(u[r0:r0 + GMLP_CHUNK, c0:c0 + SGU_GROUP_DIM] * mixed)
        gated = jnp.concatenate(tiles, axis=1)
        vb_ref[pl.ds(r0, GMLP_CHUNK), pl.ds(CONV_CH, SGU_CH)] = _rms(gated, ons_ref[...]).astype(BF16)

    o_ref[...] = x_ref[...] + _dot(vb_ref[...], wout_ref[...])


def _mix(x, norm_g, w_in, w_out, conv_w, conv_b, cln_g, cln_b, sln_g, sln_b, sgu_w, sgu_bias,
         on_conv, on_sgu, *, seq, cast_to_bf16=()):
    t, d = x.shape
    tm = MIX_TM
    n_steps = t // tm
    args = (norm_g, w_in, w_out, conv_w, conv_b, cln_g, cln_b, sln_g, sln_b, sgu_w, sgu_bias,
            on_conv, on_sgu)
    side = list(cast_to_bf16)

    def slab_spec(w):
        assert w.shape[0] % n_steps == 0
        return pl.BlockSpec((w.shape[0] // n_steps, w.shape[1]), lambda i: (i, 0))

    side_specs = [slab_spec(w) for w in side]
    outs = pl.pallas_call(
        functools.partial(_mix_kernel, blocks_per_seq=seq // tm, n_side=len(side)),
        out_shape=[jax.ShapeDtypeStruct((t, d), F32)] + [jax.ShapeDtypeStruct(w.shape, BF16) for w in side],
        grid=(n_steps,),
        in_specs=[pl.BlockSpec((tm, d), lambda i: (i, 0))] + [_resident(a) for a in args] + side_specs,
        out_specs=[pl.BlockSpec((tm, d), lambda i: (i, 0))] + side_specs,
        scratch_shapes=[
            pltpu.VMEM((CONV_CH // LANES, EXT_PITCH * (HALO + tm), LANES), F32),
            pltpu.VMEM((tm, CONV_CH), F32),
            pltpu.VMEM((tm, CONV_CH + SGU_CH), BF16),
        ],
        compiler_params=pltpu.CompilerParams(
            dimension_semantics=("arbitrary",), vmem_limit_bytes=VMEM_LIMIT),
        name="mix",
    )(x, *args, *side)
    return outs[0], outs[1:]


def _kv_kernel(m_ref, g_ref, w_ref, o_ref, h_ref):
    @pl.when(pl.program_id(0) == 0)
    def _():
        h_ref[...] = _rms(m_ref[...], g_ref[...]).astype(BF16)
    o_ref[...] = _dot(h_ref[...], w_ref[...].astype(BF16)).astype(BF16)


def _kv(mem, norm_g, w_kv):
    m, d = mem.shape
    n = w_kv.shape[1]
    return pl.pallas_call(
        _kv_kernel,
        out_shape=jax.ShapeDtypeStruct((m, n), BF16),
        grid=(n // KV_TN,),
        in_specs=[
            pl.BlockSpec((m, d), lambda j: (0, 0)),
            pl.BlockSpec((1, d), lambda j: (0, 0)),
            pl.BlockSpec((d, KV_TN), lambda j: (0, j)),
        ],
        out_specs=pl.BlockSpec((m, KV_TN), lambda j: (0, j)),
        scratch_shapes=[pltpu.VMEM((m, d), BF16)],
        compiler_params=pltpu.CompilerParams(
            dimension_semantics=("arbitrary",), vmem_limit_bytes=VMEM_LIMIT),
        name="kv",
    )(mem, norm_g, w_kv)


def _attn_kernel(x_ref, g_ref, wq_ref, k_ref, v_ref, wo_ref, o_ref, oh_ref):
    h = _rms(x_ref[...], g_ref[...]).astype(BF16)
    for hd in range(XA_HEADS):
        cols = pl.ds(hd * XA_HEAD_DIM, XA_HEAD_DIM)
        qh = _dot(h, wq_ref[:, cols]).astype(BF16)
        s = lax.dot_general(qh, k_ref[0, :, cols], (((1,), (1,)), ((), ())), preferred_element_type=F32)
        s = s * (XA_HEAD_DIM ** -0.5)
        e = jnp.exp(s - jnp.max(s, axis=-1, keepdims=True))
        p = (e / jnp.sum(e, axis=-1, keepdims=True)).astype(BF16)
        oh_ref[:, cols] = _dot(p, v_ref[0, :, cols]).astype(BF16)
    o_ref[...] = x_ref[...] + _dot(oh_ref[...], wo_ref[...])


def _attn(x, norm_g, w_q, kv, w_o, *, seq):
    t, d = x.shape
    tm = ATT_TM
    bps = seq // tm
    n_mem = kv.shape[1]
    return pl.pallas_call(
        _attn_kernel,
        out_shape=jax.ShapeDtypeStruct((t, d), F32),
        grid=(t // tm,),
        in_specs=[
            pl.BlockSpec((tm, d), lambda i: (i, 0)),
            _resident(norm_g),
            _resident(w_q),
            pl.BlockSpec((1, n_mem, d), lambda i: (i // bps, 0, 0)),
            pl.BlockSpec((1, n_mem, d), lambda i: (i // bps, 0, 1)),
            _resident(w_o),
        ],
        out_specs=pl.BlockSpec((tm, d), lambda i: (i, 0)),
        scratch_shapes=[pltpu.VMEM((tm, d), BF16)],
        compiler_params=pltpu.CompilerParams(
            dimension_semantics=("arbitrary",), vmem_limit_bytes=VMEM_LIMIT),
        name="attn",
    )(x, norm_g, w_q, kv, kv, w_o)


def kernel(x, mem, ffn1_norm, ffn1_w_in, ffn1_w_out, mix_norm, w_mix_in, conv_w, conv_b, conv_ln_g, conv_ln_b, sgu_ln_g, sgu_ln_b, sgu_w, sgu_b, out_norm_conv, out_norm_sgu, w_mix_out, xattn_norm, mem_norm, w_q, w_kv, w_o, ffn2_norm, ffn2_w_in, ffn2_w_out, final_norm):
    b, s, d = x.shape
    n_mem = mem.shape[1]
    depth = ffn1_norm.shape[0]
    row = lambda v: v.reshape(1, -1)
    ones = jnp.ones((1, d), F32)

    xs = x.reshape(b * s, d)
    mem2 = mem.reshape(b * n_mem, d)
    for l in range(depth):
        last = l == depth - 1
        xs, (w_mix_in_b, w_mix_out_b) = _ffn(
            xs, row(ffn1_norm[l]), ffn1_w_in[l], ffn1_w_out[l], ones, final_norm=False,
            cast_to_bf16=((w_mix_in[l], _glu_paired_lane_tiles()), (w_mix_out[l], None)))
        sgu_bias = jnp.repeat(jnp.transpose(sgu_b[l]), SGU_GROUP_DIM, axis=1)
        xs, (w_q_b, w_o_b) = _mix(
            xs, row(mix_norm[l]), w_mix_in_b, w_mix_out_b,
            conv_w[l], row(conv_b[l]), row(conv_ln_g[l]), row(conv_ln_b[l]),
            row(sgu_ln_g[l]), row(sgu_ln_b[l]), sgu_w[l], sgu_bias,
            row(out_norm_conv[l]), row(out_norm_sgu[l]), seq=s, cast_to_bf16=(w_q[l], w_o[l]))
        kv = _kv(mem2, row(mem_norm[l]), w_kv[l]).reshape(b, n_mem, 2 * d)
        xs = _attn(xs, row(xattn_norm[l]), w_q_b, kv, w_o_b, seq=s)
        xs, _ = _ffn(xs, row(ffn2_norm[l]), ffn2_w_in[l], ffn2_w_out[l],
                     row(final_norm) if last else ones, final_norm=last)
    return xs.reshape(b, s, d)
```
